```python
import math
import jax, jax.numpy as jnp
from jax import lax
import numpy as np

D_MODEL = 1024
BATCH = 2
SEQ = 16384
DEPTH = 1
DEC_BATCH = 32
DEC_SEQ = 64
PAST_LEN = 1024

CHUNK = 64
Q_BLOCK = 128
N_HEADS = 8
QK_NOPE = 64
QK_ROPE = 32
QK_HEAD = QK_NOPE + QK_ROPE
V_HEAD = 64
Q_LORA = 256
KV_LORA = 128
ROPE_THETA = 10000.0
ATTN_SCALE = QK_HEAD ** -0.5
POOL_WIDTH = 512
POOL_GROUPS = 4
POOL_GROUP_DIM = POOL_WIDTH // POOL_GROUPS
POOL_WINDOWS = (2, 4, 8, 16)
POOL_STATE = max(POOL_WINDOWS) - 1
POOL_OUT_GROUP = D_MODEL // POOL_GROUPS
D_FF = ((8 * D_MODEL + 767) // 768) * 256
IN_TOTAL = Q_LORA + KV_LORA + QK_ROPE + POOL_WIDTH + 2 * D_MODEL
DEEPNORM_ALPHA = (2 * DEPTH) ** 0.25
DEEPNORM_BETA = (8 * DEPTH) ** -0.25
RMS_EPS = 1e-6
LN_EPS = 1e-5
NEG_INF = -1e30

kernel_name = "hybrid_mla_pool_streaming_step"


def _rms_norm(x, g):
    xf = x.astype(jnp.float32)
    y = xf * lax.rsqrt(jnp.mean(xf * xf, axis=-1, keepdims=True) + RMS_EPS)
    return (y * g.astype(jnp.float32)).astype(x.dtype)


def _layer_norm(x, g, b):
    xf = x.astype(jnp.float32)
    mu = jnp.mean(xf, axis=-1, keepdims=True)
    var = jnp.mean(jnp.square(xf - mu), axis=-1, keepdims=True)
    y = (xf - mu) * lax.rsqrt(var + LN_EPS)
    return (y * g.astype(jnp.float32) + b.astype(jnp.float32)).astype(x.dtype)


def _rope(x, pos):
    half = QK_ROPE // 2
    inv = 1.0 / (ROPE_THETA ** (jnp.arange(half, dtype=jnp.float32) / half))
    ang = pos.astype(jnp.float32)[:, None] * inv[None, :]
    shape = (pos.shape[0],) + (1,) * (x.ndim - 3) + (half,)
    cos = jnp.cos(ang).reshape(shape)
    sin = jnp.sin(ang).reshape(shape)
    xf = x.astype(jnp.float32)
    x1, x2 = xf[..., :half], xf[..., half:]
    return jnp.concatenate([x1 * cos - x2 * sin, x2 * cos + x1 * sin], axis=-1).astype(x.dtype)


def _mixer_inputs(x, pos, w_in, b_gate, q_norm_g, w_uq, kv_norm_g, w_uk):
    b, t, _ = x.shape
    z = x @ w_in
    o1 = Q_LORA
    o2 = o1 + KV_LORA
    o3 = o2 + QK_ROPE
    o4 = o3 + POOL_WIDTH
    z_q, z_kv, z_kr, u, z_g = z[..., :o1], z[..., o1:o2], z[..., o2:o3], z[..., o3:o4], z[..., o4:]
    q = (_rms_norm(z_q, q_norm_g) @ w_uq).reshape(b, t, N_HEADS, QK_HEAD)
    q_lat = jnp.einsum('bthn,rhn->bthr', q[..., :QK_NOPE], w_uk)
    q_rope = _rope(q[..., QK_NOPE:], pos)
    c_kv = _rms_norm(z_kv, kv_norm_g)
    k_rope = _rope(z_kr, pos)
    gates = jax.nn.sigmoid((z_g + b_gate).astype(jnp.float32)).astype(x.dtype)
    return q_lat, q_rope, c_kv, k_rope, u, gates[..., :D_MODEL], gates[..., D_MODEL:]


def _mla_attend(q_lat, q_rope, c_kv, k_rope, q_pos, k_pos):
    s = (jnp.einsum('bqhr,bkr->bhqk', q_lat, c_kv)
         + jnp.einsum('bqhe,bke->bhqk', q_rope, k_rope)).astype(jnp.float32) * ATTN_SCALE
    mask = (k_pos // CHUNK)[None, :] <= (q_pos // CHUNK)[:, None]
    s = jnp.where(mask[None, None], s, NEG_INF)
    p = jax.nn.softmax(s, axis=-1).astype(c_kv.dtype)
    return jnp.einsum('bhqk,bkr->bqhr', p, c_kv)


def _prompt_attention(q_lat, q_rope, c_kv, k_rope, pos):
    b, t, h, r = q_lat.shape
    nb = t // Q_BLOCK
    ql = q_lat.reshape(b, nb, Q_BLOCK, h, r).transpose(1, 0, 2, 3, 4)
    qr = q_rope.reshape(b, nb, Q_BLOCK, h, QK_ROPE).transpose(1, 0, 2, 3, 4)
    qp = pos.reshape(nb, Q_BLOCK)

    def one_block(args):
        ql_b, qr_b, qp_b = args
        return _mla_attend(ql_b, qr_b, c_kv, k_rope, qp_b, pos)

    out = lax.map(one_block, (ql, qr, qp))
    return out.transpose(1, 0, 2, 3, 4).reshape(b, t, h, r)


def _pool_branch(u, prefix, pos, w_pool, pool_scale):
    b, t, _ = u.shape
    ext = jnp.concatenate([prefix, u], axis=1).astype(jnp.float32)
    cs = jnp.concatenate([jnp.zeros((b, 1, POOL_WIDTH), jnp.float32), jnp.cumsum(ext, axis=1)], axis=1)
    end = cs[:, POOL_STATE + 1:POOL_STATE + 1 + t]
    outs = []
    for g, w in enumerate(POOL_WINDOWS):
        sl = slice(g * POOL_GROUP_DIM, (g + 1) * POOL_GROUP_DIM)
        start = cs[:, POOL_STATE + 1 - w:POOL_STATE + 1 - w + t, sl]
        cnt = jnp.minimum(pos + 1, w).astype(jnp.float32)[None, :, None]
        d = ((end[..., sl] - start) / cnt - ext[:, POOL_STATE:, sl]).astype(u.dtype)
        outs.append(d @ w_pool[g])
    return jnp.concatenate(outs, axis=-1) * pool_scale


def _finish_layer(x, att_lat, p_branch, g_attn, g_pool, w_uv, w_attn_up, w_o,
                  ln1_g, ln1_b, w_gate, w_up, w_down, ln2_g, ln2_b):
    b, t, _ = x.shape
    o = jnp.einsum('bthr,rhv->bthv', att_lat, w_uv).reshape(b, t, N_HEADS * V_HEAD)
    a_branch = o @ w_attn_up
    m = g_attn * a_branch + g_pool * p_branch
    h = _layer_norm(DEEPNORM_ALPHA * x + m @ w_o, ln1_g, ln1_b)
    f = (jax.nn.silu(h @ w_gate) * (h @ w_up)) @ w_down
    return _layer_norm(DEEPNORM_ALPHA * h + f, ln2_g, ln2_b)


def setup_inputs(seed: int = 0) -> dict:
    key = jax.random.key(seed)
    ks = jax.random.split(key, 24)
    f32 = jnp.float32
    nrm = lambda k, s, sc: jax.random.normal(k, s, f32) * sc
    L = DEPTH
    return {
        "x_prompt": nrm(ks[0], (BATCH, SEQ, D_MODEL), 1.0),
        "x_sample": nrm(ks[1], (DEC_BATCH, DEC_SEQ, D_MODEL), 1.0),
        "cache_kv_latent": nrm(ks[2], (L, DEC_BATCH, PAST_LEN, KV_LORA), 1.0),
        "cache_k_rope": nrm(ks[3], (L, DEC_BATCH, PAST_LEN, QK_ROPE), 1.0),
        "state_pool": nrm(ks[4], (L, DEC_BATCH, POOL_STATE, POOL_WIDTH), 1.0),
        "w_in": nrm(ks[5], (L, D_MODEL, IN_TOTAL), D_MODEL ** -0.5),
        "b_gate": nrm(ks[6], (L, 2 * D_MODEL), 0.02),
        "q_norm_g": 1.0 + nrm(ks[7], (L, Q_LORA), 0.02),
        "w_uq": nrm(ks[8], (L, Q_LORA, N_HEADS * QK_HEAD), Q_LORA ** -0.5),
        "kv_norm_g": 1.0 + nrm(ks[9], (L, KV_LORA), 0.02),
        "w_uk": nrm(ks[10], (L, KV_LORA, N_HEADS, QK_NOPE), KV_LORA ** -0.5),
        "w_uv": nrm(ks[11], (L, KV_LORA, N_HEADS, V_HEAD), KV_LORA ** -0.5 * DEEPNORM_BETA),
        "w_attn_up": nrm(ks[12], (L, N_HEADS * V_HEAD, D_MODEL), (N_HEADS * V_HEAD) ** -0.5 * DEEPNORM_BETA),
        "w_pool": nrm(ks[13], (L, POOL_GROUPS, POOL_GROUP_DIM, POOL_OUT_GROUP), POOL_GROUP_DIM ** -0.5 * DEEPNORM_BETA),
        "pool_scale": 1.0 + nrm(ks[14], (L, D_MODEL), 0.02),
        "w_o": nrm(ks[15], (L, D_MODEL, D_MODEL), D_MODEL ** -0.5 * DEEPNORM_BETA),
        "ln1_g": 1.0 + nrm(ks[16], (L, D_MODEL), 0.02),
        "ln1_b": nrm(ks[17], (L, D_MODEL), 0.02),
        "w_gate": nrm(ks[18], (L, D_MODEL, D_FF), D_MODEL ** -0.5),
        "w_up": nrm(ks[19], (L, D_MODEL, D_FF), D_MODEL ** -0.5 * DEEPNORM_BETA),
        "w_down": nrm(ks[20], (L, D_FF, D_MODEL), D_FF ** -0.5 * DEEPNORM_BETA),
        "ln2_g": 1.0 + nrm(ks[21], (L, D_MODEL), 0.02),
        "ln2_b": nrm(ks[22], (L, D_MODEL), 0.02),
    }


def reference(x_prompt, x_sample, cache_kv_latent, cache_k_rope, state_pool, w_in, b_gate,
              q_norm_g, w_uq, kv_norm_g, w_uk, w_uv, w_attn_up, w_pool, pool_scale, w_o,
              ln1_g, ln1_b, w_gate, w_up, w_down, ln2_g, ln2_b):
    n_p = x_prompt.shape[1]
    n_s = x_sample.shape[1]
    past = cache_kv_latent.shape[2]
    pos_p = jnp.arange(n_p, dtype=jnp.int32)
    pos_s = past + jnp.arange(n_s, dtype=jnp.int32)
    k_pos_s = jnp.arange(past + n_s, dtype=jnp.int32)
    h_p, h_s = x_prompt, x_sample
    c_p_list, kr_p_list, pool_p_list = [], [], []
    c_s_list, kr_s_list, pool_s_list = [], [], []
    for l in range(DEPTH):
        ql, qr, c, kr, u, ga, gp = _mixer_inputs(h_p, pos_p, w_in[l], b_gate[l], q_norm_g[l],
                                                 w_uq[l], kv_norm_g[l], w_uk[l])
        att = _prompt_attention(ql, qr, c, kr, pos_p)
        prefix = jnp.zeros((h_p.shape[0], POOL_STATE, POOL_WIDTH), u.dtype)
        pb = _pool_branch(u, prefix, pos_p, w_pool[l], pool_scale[l])
        c_p_list.append(c)
        kr_p_list.append(kr)
        pool_p_list.append(u[:, -POOL_STATE:])
        h_p = _finish_layer(h_p, att, pb, ga, gp, w_uv[l], w_attn_up[l], w_o[l], ln1_g[l], ln1_b[l],
                            w_gate[l], w_up[l], w_down[l], ln2_g[l], ln2_b[l])
        ql, qr, c, kr, u, ga, gp = _mixer_inputs(h_s, pos_s, w_in[l], b_gate[l], q_norm_g[l],
                                                 w_uq[l], kv_norm_g[l], w_uk[l])
        c_all = jnp.concatenate([cache_kv_latent[l], c], axis=1)
        kr_all = jnp.concatenate([cache_k_rope[l], kr], axis=1)
        att = _mla_attend(ql, qr, c_all, kr_all, pos_s, k_pos_s)
        pb = _pool_branch(u, state_pool[l], pos_s, w_pool[l], pool_scale[l])
        u_all = jnp.concatenate([state_pool[l], u], axis=1)
        c_s_list.append(c)
        kr_s_list.append(kr)
        pool_s_list.append(u_all[:, -POOL_STATE:])
        h_s = _finish_layer(h_s, att, pb, ga, gp, w_uv[l], w_attn_up[l], w_o[l], ln1_g[l], ln1_b[l],
                            w_gate[l], w_up[l], w_down[l], ln2_g[l], ln2_b[l])
    return (h_p, h_s, jnp.stack(c_p_list), jnp.stack(kr_p_list), jnp.stack(pool_p_list),
            jnp.stack(c_s_list), jnp.stack(kr_s_list), jnp.stack(pool_s_list))
```

```python
import functools
import math

import jax
import jax.numpy as jnp
from jax import lax
from jax.experimental import pallas as pl
from jax.experimental.pallas import tpu as pltpu

F32 = jnp.float32
BF16 = jnp.bfloat16

N_HEADS = 8
QK_NOPE = 64
QK_ROPE = 32
QK_HEAD = QK_NOPE + QK_ROPE
V_HEAD = 64
Q_LORA = 256
KV_LORA = 128
QK_CAT = KV_LORA + QK_ROPE
Q_ROWS = KV_LORA + 2 * QK_ROPE
CHUNK = 64
CHUNK_SHIFT = 6
ROPE_THETA = 10000.0
ATTN_SCALE = QK_HEAD ** -0.5
POOL_WIDTH = 512
POOL_GROUPS = 4
POOL_GROUP_DIM = POOL_WIDTH // POOL_GROUPS
POOL_WINDOWS = (2, 4, 8, 16)
POOL_STATE = max(POOL_WINDOWS) - 1
POOL_HALO = POOL_STATE + 1
RMS_EPS = 1e-6
LN_EPS = 1e-5
NEG_INF = -1e30
Q_SCALE = ATTN_SCALE * math.log2(math.e)

VMEM_LIMIT_BYTES = 56 * 1024 * 1024

TOKEN_TILE = 512
KV_TILE = 512
Q_TILE = 512
FF_CHUNKS = (768, 768, 768, 512)


def _const_spec(shape):
    nd = len(shape)
    return pl.BlockSpec(shape, lambda *_: (0,) * nd, pipeline_mode=pl.Buffered(1))


def _params(*sem):
    return pltpu.CompilerParams(dimension_semantics=sem, vmem_limit_bytes=VMEM_LIMIT_BYTES)


def _fold_q_kernel(wuk_ref, wuqn_ref, wuqr_ref, out_ref):
    half = QK_ROPE // 2
    for h in range(N_HEADS):
        lat = jnp.dot(wuk_ref[h], wuqn_ref[h], preferred_element_type=F32,
                      precision=lax.Precision.HIGHEST)
        rope = wuqr_ref[h]
        out_ref[h, 0:KV_LORA, :] = lat.astype(BF16)
        out_ref[h, KV_LORA:KV_LORA + QK_ROPE, :] = rope.astype(BF16)
        out_ref[h, KV_LORA + QK_ROPE:KV_LORA + QK_ROPE + half, :] = (-rope[half:]).astype(BF16)
        out_ref[h, KV_LORA + QK_ROPE + half:Q_ROWS, :] = rope[:half].astype(BF16)


def _fold_q(w_uq, w_uk):
    wq = w_uq.reshape(Q_LORA, N_HEADS, QK_HEAD)
    wuqn_t = jnp.transpose(wq[:, :, :QK_NOPE], (1, 2, 0))
    wuqr_t = jnp.transpose(wq[:, :, QK_NOPE:], (1, 2, 0))
    wuk_t = jnp.transpose(w_uk, (1, 0, 2))
    out = pl.pallas_call(
        _fold_q_kernel,
        out_shape=jax.ShapeDtypeStruct((N_HEADS, Q_ROWS, Q_LORA), BF16),
        name="fold_q",
    )(wuk_t, wuqn_t, wuqr_t)
    return out.reshape(N_HEADS * Q_ROWS, Q_LORA)


def _mixer_in_kernel(x_ref, w1_ref, wq_ref, qg_ref, kvg_ref, cosr_ref, sinr_ref, cost_ref, sint_ref,
                     qT_ref, kcat_ref, cT_ref, ckv_ref, kr_ref, u_ref):
    xb = x_ref[...].astype(BF16)
    z = jnp.dot(xb, w1_ref[...], preferred_element_type=F32)
    o1 = Q_LORA
    o2 = o1 + KV_LORA
    o3 = o2 + POOL_WIDTH
    o4 = o3 + QK_ROPE
    zq, zkv, u = z[:, :o1], z[:, o1:o2], z[:, o2:o3]
    zkr, zkr_rot = z[:, o3:o4], z[:, o4:o4 + QK_ROPE]
    u_ref[0] = u

    c = zkv * lax.rsqrt(jnp.mean(zkv * zkv, axis=-1, keepdims=True) + RMS_EPS) * kvg_ref[...]
    ckv_ref[0] = c
    kr = zkr * cosr_ref[...] + zkr_rot * sinr_ref[...]
    kr_ref[0] = kr
    kcat_ref[0, :, 0:KV_LORA] = c.astype(BF16)
    kcat_ref[0, :, KV_LORA:QK_CAT] = kr.astype(BF16)
    cT_ref[0, 0] = c.T.astype(BF16)

    zqn = zq * lax.rsqrt(jnp.mean(zq * zq, axis=-1, keepdims=True) + RMS_EPS) * (qg_ref[...] * Q_SCALE)
    zqn_t = zqn.T.astype(BF16)
    q_all = jnp.dot(wq_ref[...], zqn_t, preferred_element_type=F32)
    cos_t = cost_ref[...]
    sin_t = sint_ref[...]
    for h in range(N_HEADS):
        b = h * Q_ROWS
        qT_ref[0, h, 0:KV_LORA, :] = q_all[b:b + KV_LORA].astype(BF16)
        rope = (q_all[b + KV_LORA:b + KV_LORA + QK_ROPE] * cos_t
                + q_all[b + KV_LORA + QK_ROPE:b + Q_ROWS] * sin_t)
        qT_ref[0, h, KV_LORA:QK_CAT, :] = rope.astype(BF16)


def _mixer_in(x, pos, w1, wq, qg, kvg):
    bsz, t, d = x.shape
    tm = TOKEN_TILE
    nt = t // tm
    half = QK_ROPE // 2
    inv = 1.0 / (ROPE_THETA ** (jnp.arange(half, dtype=F32) / half))
    ang = pos.astype(F32)[:, None] * inv[None, :]
    cos_r = jnp.tile(jnp.cos(ang), (1, 2))
    sin_r = jnp.tile(jnp.sin(ang), (1, 2))
    cos_t, sin_t = cos_r.T, sin_r.T
    n1 = w1.shape[1]
    outs = pl.pallas_call(
        _mixer_in_kernel,
        grid=(bsz, nt),
        in_specs=[
            pl.BlockSpec((None, tm, d), lambda b, i: (b, i, 0)),
            _const_spec((d, n1)),
            _const_spec((N_HEADS * Q_ROWS, Q_LORA)),
            _const_spec((1, Q_LORA)),
            _const_spec((1, KV_LORA)),
            pl.BlockSpec((tm, QK_ROPE), lambda b, i: (i, 0)),
            pl.BlockSpec((tm, QK_ROPE), lambda b, i: (i, 0)),
            pl.BlockSpec((QK_ROPE, tm), lambda b, i: (0, i)),
            pl.BlockSpec((QK_ROPE, tm), lambda b, i: (0, i)),
        ],
        out_specs=[
            pl.BlockSpec((1, N_HEADS, QK_CAT, tm), lambda b, i: (b, 0, 0, i)),
            pl.BlockSpec((1, tm, QK_CAT), lambda b, i: (b, i, 0)),
            pl.BlockSpec((1, 1, KV_LORA, tm), lambda b, i: (b, i, 0, 0)),
            pl.BlockSpec((1, tm, KV_LORA), lambda b, i: (b, i, 0)),
            pl.BlockSpec((1, tm, QK_ROPE), lambda b, i: (b, i, 0)),
            pl.BlockSpec((1, tm, POOL_WIDTH), lambda b, i: (b, i, 0)),
        ],
        out_shape=[
            jax.ShapeDtypeStruct((bsz, N_HEADS, QK_CAT, t), BF16),
            jax.ShapeDtypeStruct((bsz, t, QK_CAT), BF16),
            jax.ShapeDtypeStruct((bsz, nt, KV_LORA, tm), BF16),
            jax.ShapeDtypeStruct((bsz, t, KV_LORA), F32),
            jax.ShapeDtypeStruct((bsz, t, QK_ROPE), F32),
            jax.ShapeDtypeStruct((bsz, t, POOL_WIDTH), F32),
        ],
        compiler_params=_params("parallel", "parallel"),
        name="mixer_in",
    )(x, w1, wq, qg, kvg, cos_r, sin_r, cos_t, sin_t)
    return outs


def _attention_kernel(qT_ref, kcat_ref, cT_ref, wuvT_ref, oT_ref, m_ref, l_ref, acc_ref,
                      *, tq, tk, q_offset, n_keys):
    i = pl.program_id(1)
    q0 = q_offset + i * tq
    first_end = (q0 // CHUNK + 1) * CHUNK
    last_end = ((q0 + tq - 1) // CHUNK + 1) * CHUNK
    n_all = (jnp.minimum(last_end, n_keys) + tk - 1) // tk
    n_full = jnp.minimum(first_end // tk, n_all)

    m_ref[...] = jnp.full(m_ref.shape, NEG_INF, F32)
    l_ref[...] = jnp.zeros(l_ref.shape, F32)
    acc_ref[...] = jnp.zeros(acc_ref.shape, F32)

    def step(j, masked):
        k = kcat_ref[0, pl.ds(pl.multiple_of(j * tk, tk), tk), :]
        c_t = cT_ref[0, j]
        if masked:
            kchunk = lax.shift_right_arithmetic(j * tk + lax.broadcasted_iota(jnp.int32, (tk, 1), 0), CHUNK_SHIFT)
            qchunk = lax.shift_right_arithmetic(q0 + lax.broadcasted_iota(jnp.int32, (1, tq), 1), CHUNK_SHIFT)
            visible = kchunk <= qchunk
        for h in range(N_HEADS):
            s = jnp.dot(k, qT_ref[0, h], preferred_element_type=F32)
            if masked:
                s = jnp.where(visible, s, NEG_INF)
            m_prev = m_ref[h:h + 1, :]
            m_new = jnp.maximum(m_prev, jnp.max(s, axis=0, keepdims=True))
            alpha = jnp.exp2(m_prev - m_new)
            p = jnp.exp2(s - m_new)
            l_ref[h:h + 1, :] = alpha * l_ref[h:h + 1, :] + jnp.sum(p, axis=0, keepdims=True)
            pv = jnp.dot(c_t, p.astype(BF16), preferred_element_type=F32)
            acc_ref[h] = alpha * acc_ref[h] + pv
            m_ref[h:h + 1, :] = m_new

    def full_body(j, carry):
        step(j, False)
        return carry

    def masked_body(j, carry):
        step(j, True)
        return carry

    lax.fori_loop(0, n_full, full_body, 0)
    lax.fori_loop(n_full, n_all, masked_body, 0)

    for h in range(N_HEADS):
        att_t = (acc_ref[h] / l_ref[h:h + 1, :]).astype(BF16)
        o_t = jnp.dot(wuvT_ref[h], att_t, preferred_element_type=F32)
        oT_ref[0, h * V_HEAD:(h + 1) * V_HEAD, :] = o_t.astype(BF16)


def _attention(qT, kcat, cT, wuvT, *, tq, tk, q_offset, n_keys):
    bsz, _, _, t_q = qT.shape
    t_k = kcat.shape[1]
    kern = functools.partial(_attention_kernel, tq=tq, tk=tk, q_offset=q_offset, n_keys=n_keys)
    return pl.pallas_call(
        kern,
        grid=(bsz, t_q // tq),
        in_specs=[
            pl.BlockSpec((1, N_HEADS, QK_CAT, tq), lambda b, i: (b, 0, 0, i)),
            pl.BlockSpec((1, t_k, QK_CAT), lambda b, i: (b, 0, 0)),
            pl.BlockSpec((1, t_k // tk, KV_LORA, tk), lambda b, i: (b, 0, 0, 0)),
            _const_spec((N_HEADS, V_HEAD, KV_LORA)),
        ],
        out_specs=pl.BlockSpec((1, N_HEADS * V_HEAD, tq), lambda b, i: (b, 0, i)),
        out_shape=jax.ShapeDtypeStruct((bsz, N_HEADS * V_HEAD, t_q), BF16),
        scratch_shapes=[
            pltpu.VMEM((N_HEADS, tq), F32),
            pltpu.VMEM((N_HEADS, tq), F32),
            pltpu.VMEM((N_HEADS, KV_LORA, tq), F32),
        ],
        compiler_params=_params("parallel", "arbitrary"),
        name="attention",
    )(qT, kcat, cT, wuvT)


def _layer_norm(r, g, b):
    mu = jnp.mean(r, axis=-1, keepdims=True)
    cen = r - mu
    var = jnp.mean(cen * cen, axis=-1, keepdims=True)
    return cen * lax.rsqrt(var + LN_EPS) * g + b


def _mix_kernel(x_ref, oT_ref, u_ref, pre_ref, wg_ref, bg_ref, wup_ref, wpool_ref, pscale_ref, wo_ref,
                g1_ref, b1_ref, h_ref, ext_ref, d_ref, *, seg_len, n_seg, pos0_fn, alpha):
    d_model = x_ref.shape[-1]
    x = x_ref[...]
    xb = x.astype(BF16)
    zg = jnp.dot(xb, wg_ref[...], preferred_element_type=F32) + bg_ref[...]
    gates = 1.0 / (1.0 + jnp.exp(-zg))

    pos0 = pos0_fn(pl.program_id(0))
    row = lax.broadcasted_iota(jnp.int32, (seg_len, 1), 0)
    for s in range(n_seg):
        ext_ref[0:POOL_HALO, :] = pre_ref[s]
        ext_ref[POOL_HALO:POOL_HALO + seg_len, :] = u_ref[s * seg_len:(s + 1) * seg_len, :]
        for g, w in enumerate(POOL_WINDOWS):
            lanes = slice(g * POOL_GROUP_DIM, (g + 1) * POOL_GROUP_DIM)
            cur = ext_ref[POOL_HALO:POOL_HALO + seg_len, lanes]
            acc = cur
            for k in range(1, w):
                acc = acc + ext_ref[POOL_HALO - k:POOL_HALO - k + seg_len, lanes]
            cnt = jnp.minimum(pos0 + row + 1, w).astype(F32)
            d_ref[s * seg_len:(s + 1) * seg_len, lanes] = (acc * (1.0 / cnt) - cur).astype(BF16)

    pool_out = d_model // POOL_GROUPS
    parts = []
    for g in range(POOL_GROUPS):
        dg = d_ref[:, g * POOL_GROUP_DIM:(g + 1) * POOL_GROUP_DIM]
        parts.append(jnp.dot(dg, wpool_ref[g], preferred_element_type=F32))
    p_branch = jnp.concatenate(parts, axis=-1) * pscale_ref[...]

    a_branch = lax.dot_general(oT_ref[...], wup_ref[...], (((0,), (0,)), ((), ())),
                               preferred_element_type=F32)
    m = gates[:, :d_model] * a_branch + gates[:, d_model:] * p_branch
    r = alpha * x + jnp.dot(m.astype(BF16), wo_ref[...], preferred_element_type=F32)
    h_ref[...] = _layer_norm(r, g1_ref[...], b1_ref[...])


def _mix(x2, oT, u2, prefix, wg, bg, wup, wpool, pscale, wo, g1, b1, *, seg_len, pos0_fn, alpha):
    n, d = x2.shape
    tm = TOKEN_TILE
    n_seg = tm // seg_len
    _, hv, t_o = oT.shape
    tiles_per_row = t_o // tm
    kern = functools.partial(_mix_kernel, seg_len=seg_len, n_seg=n_seg, pos0_fn=pos0_fn, alpha=alpha)
    return pl.pallas_call(
        kern,
        grid=(n // tm,),
        in_specs=[
            pl.BlockSpec((tm, d), lambda i: (i, 0)),
            pl.BlockSpec((None, hv, tm), lambda i: (i // tiles_per_row, 0, i % tiles_per_row)),
            pl.BlockSpec((tm, POOL_WIDTH), lambda i: (i, 0)),
            pl.BlockSpec((n_seg, POOL_HALO, POOL_WIDTH), lambda i: (i, 0, 0)),
            _const_spec(wg.shape), _const_spec(bg.shape), _const_spec(wup.shape),
            _const_spec(wpool.shape), _const_spec(pscale.shape), _const_spec(wo.shape),
            _const_spec(g1.shape), _const_spec(b1.shape),
        ],
        out_specs=pl.BlockSpec((tm, d), lambda i: (i, 0)),
        out_shape=jax.ShapeDtypeStruct((n, d), F32),
        scratch_shapes=[
            pltpu.VMEM((POOL_HALO + seg_len, POOL_WIDTH), F32),
            pltpu.VMEM((tm, POOL_WIDTH), BF16),
        ],
        compiler_params=_params("parallel"),
        name="mix",
    )(x2, oT, u2, prefix, wg, bg, wup, wpool, pscale, wo, g1, b1)


def _ffn_kernel(h_ref, wgate_ref, wupf_ref, wdown_ref, g2_ref, b2_ref, y_ref, *, alpha):
    h = h_ref[...]
    hb = h.astype(BF16)
    f = None
    lo = 0
    for width in FF_CHUNKS:
        gate = jnp.dot(hb, wgate_ref[:, lo:lo + width], preferred_element_type=F32)
        up = jnp.dot(hb, wupf_ref[:, lo:lo + width], preferred_element_type=F32)
        act = (gate / (1.0 + jnp.exp(-gate)) * up).astype(BF16)
        part = jnp.dot(act, wdown_ref[lo:lo + width, :], preferred_element_type=F32)
        f = part if f is None else f + part
        lo += width
    y_ref[...] = _layer_norm(alpha * h + f, g2_ref[...], b2_ref[...])


def _ffn(h2, wgate, wupf, wdown, g2, b2, *, alpha):
    n, d = h2.shape
    tm = TOKEN_TILE
    assert sum(FF_CHUNKS) == wgate.shape[1]
    return pl.pallas_call(
        functools.partial(_ffn_kernel, alpha=alpha),
        grid=(n // tm,),
        in_specs=[
            pl.BlockSpec((tm, d), lambda i: (i, 0)),
            _const_spec(wgate.shape), _const_spec(wupf.shape), _const_spec(wdown.shape),
            _const_spec(g2.shape), _const_spec(b2.shape),
        ],
        out_specs=pl.BlockSpec((tm, d), lambda i: (i, 0)),
        out_shape=jax.ShapeDtypeStruct((n, d), F32),
        compiler_params=_params("parallel"),
        name="ffn",
    )(h2, wgate, wupf, wdown, g2, b2)


def _prompt_pos0(tiles_per_seq, i):
    return (i % tiles_per_seq) * TOKEN_TILE


def _sample_pos0(past, i):
    return past


def kernel(x_prompt, x_sample, cache_kv_latent, cache_k_rope, state_pool, w_in, b_gate, q_norm_g, w_uq,
           kv_norm_g, w_uk, w_uv, w_attn_up, w_pool, pool_scale, w_o, ln1_g, ln1_b, w_gate, w_up, w_down,
           ln2_g, ln2_b):
    depth = w_in.shape[0]
    bp, n_p, d = x_prompt.shape
    bs, n_s, _ = x_sample.shape
    past = cache_kv_latent.shape[2]
    alpha = (2 * depth) ** 0.25
    half = QK_ROPE // 2
    o1 = Q_LORA
    o2 = o1 + KV_LORA
    o3 = o2 + QK_ROPE
    o4 = o3 + POOL_WIDTH

    pos_p = jnp.arange(n_p, dtype=jnp.int32)
    pos_s = past + (jnp.arange(bs * n_s, dtype=jnp.int32) % n_s)
    n_keys_s = past + n_s
    keys_pad_s = -(-n_keys_s // 128) * 128

    h_p, h_s = x_prompt, x_sample
    c_p_list, kr_p_list, pool_p_list = [], [], []
    c_s_list, kr_s_list, pool_s_list = [], [], []
    for l in range(depth):
        wl = w_in[l]
        w_kr = wl[:, o2:o3]
        w_kr_rot = jnp.concatenate([-w_kr[:, half:], w_kr[:, :half]], axis=1)
        w1 = jnp.concatenate([wl[:, :o2], wl[:, o3:o4], w_kr, w_kr_rot], axis=1).astype(BF16)
        wg = wl[:, o4:].astype(BF16)
        bg = b_gate[l][None, :]
        wq = _fold_q(w_uq[l], w_uk[l])
        qg = q_norm_g[l][None, :]
        kvg = kv_norm_g[l][None, :]
        wuvT = jnp.transpose(w_uv[l], (1, 2, 0)).astype(BF16)
        wup = w_attn_up[l].astype(BF16)
        wpool = w_pool[l].astype(BF16)
        pscale = pool_scale[l][None, :]
        wo = w_o[l].astype(BF16)
        g1, b1 = ln1_g[l][None, :], ln1_b[l][None, :]
        g2, b2 = ln2_g[l][None, :], ln2_b[l][None, :]
        wgate, wupf, wdown = w_gate[l].astype(BF16), w_up[l].astype(BF16), w_down[l].astype(BF16)

        qT, kcat, cT, c, kr, u = _mixer_in(h_p, pos_p, w1, wq, qg, kvg)
        oT = _attention(qT, kcat, cT, wuvT, tq=Q_TILE, tk=KV_TILE, q_offset=0, n_keys=n_p)
        tiles_per_seq = n_p // TOKEN_TILE
        tails = u.reshape(bp, tiles_per_seq, TOKEN_TILE, POOL_WIDTH)[:, :, TOKEN_TILE - POOL_HALO:, :]
        prefix = jnp.concatenate([jnp.zeros_like(tails[:, :1]), tails[:, :-1]], axis=1)
        prefix = prefix.reshape(bp * tiles_per_seq, POOL_HALO, POOL_WIDTH)
        hmid = _mix(h_p.reshape(bp * n_p, d), oT, u.reshape(bp * n_p, POOL_WIDTH), prefix,
                    wg, bg, wup, wpool, pscale, wo, g1, b1, seg_len=TOKEN_TILE,
                    pos0_fn=functools.partial(_prompt_pos0, tiles_per_seq), alpha=alpha)
        y = _ffn(hmid, wgate, wupf, wdown, g2, b2, alpha=alpha)
        c_p_list.append(c)
        kr_p_list.append(kr)
        pool_p_list.append(u[:, n_p - POOL_STATE:])
        h_p = y.reshape(bp, n_p, d)

        xs = h_s.reshape(1, bs * n_s, d)
        qT, _, _, c, kr, u = _mixer_in(xs, pos_s, w1, wq, qg, kvg)
        c = c.reshape(bs, n_s, KV_LORA)
        kr = kr.reshape(bs, n_s, QK_ROPE)
        u = u.reshape(bs, n_s, POOL_WIDTH)
        qT = jnp.transpose(qT.reshape(N_HEADS, QK_CAT, bs, n_s), (2, 0, 1, 3))
        c_all = jnp.concatenate([cache_kv_latent[l], c], axis=1)
        kr_all = jnp.concatenate([cache_k_rope[l], kr], axis=1)
        kcat = jnp.concatenate([c_all, kr_all], axis=-1).astype(BF16)
        kcat = jnp.pad(kcat, ((0, 0), (0, keys_pad_s - n_keys_s), (0, 0)))
        cT = jnp.transpose(jnp.pad(c_all, ((0, 0), (0, keys_pad_s - n_keys_s), (0, 0))), (0, 2, 1))
        cT = cT.astype(BF16)[:, None]
        oT = _attention(qT, kcat, cT, wuvT, tq=n_s, tk=keys_pad_s, q_offset=past, n_keys=n_keys_s)
        oT2 = jnp.transpose(oT, (1, 0, 2)).reshape(1, N_HEADS * V_HEAD, bs * n_s)
        prefix = jnp.pad(state_pool[l], ((0, 0), (POOL_HALO - POOL_STATE, 0), (0, 0)))
        hmid = _mix(h_s.reshape(bs * n_s, d), oT2, u.reshape(bs * n_s, POOL_WIDTH), prefix,
                    wg, bg, wup, wpool, pscale, wo, g1, b1, seg_len=n_s,
                    pos0_fn=functools.partial(_sample_pos0, past), alpha=alpha)
        y = _ffn(hmid, wgate, wupf, wdown, g2, b2, alpha=alpha)
        u_all = jnp.concatenate([state_pool[l], u], axis=1)
        c_s_list.append(c)
        kr_s_list.append(kr)
        pool_s_list.append(u_all[:, u_all.shape[1] - POOL_STATE:])
        h_s = y.reshape(bs, n_s, d)

    return (h_p, h_s, jnp.stack(c_p_list), jnp.stack(kr_p_list), jnp.stack(pool_p_list),
            jnp.stack(c_s_list), jnp.stack(kr_s_list), jnp.stack(pool_s_list))
```

```python
import functools
import math

import jax
import jax.numpy as jnp
from jax import lax
from jax.experimental import pallas as pl
from jax.experimental.pallas import tpu as pltpu

F32 = jnp.float32
BF16 = jnp.bfloat16

N_HEADS = 8
QK_NOPE = 64
QK_ROPE = 32
QK_HEAD = QK_NOPE + QK_ROPE
V_HEAD = 64
Q_LORA = 256
KV_LORA = 128
QK_CAT = KV_LORA + QK_ROPE
Q_ROWS = KV_LORA + 2 * QK_ROPE
V_ROWS = KV_LORA + 16
CHUNK = 64
CHUNK_SHIFT = 6
ROPE_THETA = 10000.0
ATTN_SCALE = QK_HEAD ** -0.5
POOL_WIDTH = 512
POOL_GROUPS = 4
POOL_GROUP_DIM = POOL_WIDTH // POOL_GROUPS
POOL_WINDOWS = (2, 4, 8, 16)
POOL_STATE = max(POOL_WINDOWS) - 1
POOL_HALO = POOL_STATE + 1
RMS_EPS = 1e-6
LN_EPS = 1e-5
NEG_INF = -1e30
Q_SCALE = ATTN_SCALE * math.log2(math.e)

VMEM_LIMIT_BYTES = 56 * 1024 * 1024

TOKEN_TILE = 512
KV_TILE = 512
Q_TILE = 512
FF_CHUNKS = (768, 768, 768, 512)


def _const_spec(shape):
    nd = len(shape)
    return pl.BlockSpec(shape, lambda *_: (0,) * nd, pipeline_mode=pl.Buffered(1))


def _params(*sem):
    return pltpu.CompilerParams(dimension_semantics=sem, vmem_limit_bytes=VMEM_LIMIT_BYTES)


def _fold_q_kernel(wuk_ref, wuqn_ref, wuqr_ref, out_ref):
    half = QK_ROPE // 2
    for h in range(N_HEADS):
        lat = jnp.dot(wuk_ref[h], wuqn_ref[h], preferred_element_type=F32,
                      precision=lax.Precision.HIGHEST)
        rope = wuqr_ref[h]
        out_ref[h, 0:KV_LORA, :] = lat.astype(BF16)
        out_ref[h, KV_LORA:KV_LORA + QK_ROPE, :] = rope.astype(BF16)
        out_ref[h, KV_LORA + QK_ROPE:KV_LORA + QK_ROPE + half, :] = (-rope[half:]).astype(BF16)
        out_ref[h, KV_LORA + QK_ROPE + half:Q_ROWS, :] = rope[:half].astype(BF16)


def _fold_q(w_uq, w_uk):
    wq = w_uq.reshape(Q_LORA, N_HEADS, QK_HEAD)
    wuqn_t = jnp.transpose(wq[:, :, :QK_NOPE], (1, 2, 0))
    wuqr_t = jnp.transpose(wq[:, :, QK_NOPE:], (1, 2, 0))
    wuk_t = jnp.transpose(w_uk, (1, 0, 2))
    out = pl.pallas_call(
        _fold_q_kernel,
        out_shape=jax.ShapeDtypeStruct((N_HEADS, Q_ROWS, Q_LORA), BF16),
        name="fold_q",
    )(wuk_t, wuqn_t, wuqr_t)
    return out.reshape(N_HEADS * Q_ROWS, Q_LORA)


def _mixer_in_kernel(x_ref, w1_ref, wq_ref, qg_ref, kvg_ref, cosr_ref, sinr_ref, cost_ref, sint_ref,
                     qT_ref, kcat_ref, cT_ref, ckv_ref, kr_ref, u_ref):
    xb = x_ref[...].astype(BF16)
    z = jnp.dot(xb, w1_ref[...], preferred_element_type=F32)
    o1 = Q_LORA
    o2 = o1 + KV_LORA
    o3 = o2 + POOL_WIDTH
    o4 = o3 + QK_ROPE
    zq, zkv, u = z[:, :o1], z[:, o1:o2], z[:, o2:o3]
    zkr, zkr_rot = z[:, o3:o4], z[:, o4:o4 + QK_ROPE]
    u_ref[0] = u

    c = zkv * lax.rsqrt(jnp.mean(zkv * zkv, axis=-1, keepdims=True) + RMS_EPS) * kvg_ref[...]
    ckv_ref[0] = c
    kr = zkr * cosr_ref[...] + zkr_rot * sinr_ref[...]
    kr_ref[0] = kr
    kcat_ref[0, :, 0:KV_LORA] = c.astype(BF16)
    kcat_ref[0, :, KV_LORA:QK_CAT] = kr.astype(BF16)
    cT_ref[0, 0, 0:KV_LORA, :] = c.T.astype(BF16)
    ones_row = lax.broadcasted_iota(jnp.int32, (V_ROWS - KV_LORA, c.shape[0]), 0) == 0
    cT_ref[0, 0, KV_LORA:V_ROWS, :] = jnp.where(ones_row, 1.0, 0.0).astype(BF16)

    zqn = zq * lax.rsqrt(jnp.mean(zq * zq, axis=-1, keepdims=True) + RMS_EPS) * (qg_ref[...] * Q_SCALE)
    zqn_t = zqn.T.astype(BF16)
    q_all = jnp.dot(wq_ref[...], zqn_t, preferred_element_type=F32)
    cos_t = cost_ref[...]
    sin_t = sint_ref[...]
    for h in range(N_HEADS):
        b = h * Q_ROWS
        qT_ref[0, h, 0:KV_LORA, :] = q_all[b:b + KV_LORA].astype(BF16)
        rope = (q_all[b + KV_LORA:b + KV_LORA + QK_ROPE] * cos_t
                + q_all[b + KV_LORA + QK_ROPE:b + Q_ROWS] * sin_t)
        qT_ref[0, h, KV_LORA:QK_CAT, :] = rope.astype(BF16)


def _mixer_in(x, pos, w1, wq, qg, kvg):
    bsz, t, d = x.shape
    tm = TOKEN_TILE
    nt = t // tm
    half = QK_ROPE // 2
    inv = 1.0 / (ROPE_THETA ** (jnp.arange(half, dtype=F32) / half))
    ang = pos.astype(F32)[:, None] * inv[None, :]
    cos_r = jnp.tile(jnp.cos(ang), (1, 2))
    sin_r = jnp.tile(jnp.sin(ang), (1, 2))
    cos_t, sin_t = cos_r.T, sin_r.T
    n1 = w1.shape[1]
    outs = pl.pallas_call(
        _mixer_in_kernel,
        grid=(bsz, nt),
        in_specs=[
            pl.BlockSpec((None, tm, d), lambda b, i: (b, i, 0)),
            _const_spec((d, n1)),
            _const_spec((N_HEADS * Q_ROWS, Q_LORA)),
            _const_spec((1, Q_LORA)),
            _const_spec((1, KV_LORA)),
            pl.BlockSpec((tm, QK_ROPE), lambda b, i: (i, 0)),
            pl.BlockSpec((tm, QK_ROPE), lambda b, i: (i, 0)),
            pl.BlockSpec((QK_ROPE, tm), lambda b, i: (0, i)),
            pl.BlockSpec((QK_ROPE, tm), lambda b, i: (0, i)),
        ],
        out_specs=[
            pl.BlockSpec((1, N_HEADS, QK_CAT, tm), lambda b, i: (b, 0, 0, i)),
            pl.BlockSpec((1, tm, QK_CAT), lambda b, i: (b, i, 0)),
            pl.BlockSpec((1, 1, V_ROWS, tm), lambda b, i: (b, i, 0, 0)),
            pl.BlockSpec((1, tm, KV_LORA), lambda b, i: (b, i, 0)),
            pl.BlockSpec((1, tm, QK_ROPE), lambda b, i: (b, i, 0)),
            pl.BlockSpec((1, tm, POOL_WIDTH), lambda b, i: (b, i, 0)),
        ],
        out_shape=[
            jax.ShapeDtypeStruct((bsz, N_HEADS, QK_CAT, t), BF16),
            jax.ShapeDtypeStruct((bsz, t, QK_CAT), BF16),
            jax.ShapeDtypeStruct((bsz, nt, V_ROWS, tm), BF16),
            jax.ShapeDtypeStruct((bsz, t, KV_LORA), F32),
            jax.ShapeDtypeStruct((bsz, t, QK_ROPE), F32),
            jax.ShapeDtypeStruct((bsz, t, POOL_WIDTH), F32),
        ],
        compiler_params=_params("parallel", "parallel"),
        name="mixer_in",
    )(x, w1, wq, qg, kvg, cos_r, sin_r, cos_t, sin_t)
    return outs


def _attention_kernel(qT_ref, kcat_ref, cT_ref, wuvT_ref, oT_ref, m_ref, alpha_ref, acc_ref,
                      s_ref, mt_ref, p_ref, *, tq, tk, q_offset, n_keys, n_tiles):
    i = pl.program_id(1)
    q0 = q_offset + i * tq
    first_end = (q0 // CHUNK + 1) * CHUNK
    last_end = ((q0 + tq - 1) // CHUNK + 1) * CHUNK
    n_all = (jnp.minimum(last_end, n_keys) + tk - 1) // tk
    n_full = jnp.minimum(first_end // tk, n_all)

    m_ref[...] = jnp.full(m_ref.shape, NEG_INF, F32)
    acc_ref[...] = jnp.zeros(acc_ref.shape, F32)

    def stage_a(j, h, slot):
        k = kcat_ref[0, pl.ds(pl.multiple_of(j * tk, tk), tk), :]
        s = jnp.dot(k, qT_ref[0, h], preferred_element_type=F32)
        s_ref[slot] = s
        mt_ref[slot] = jnp.max(s, axis=0, keepdims=True)

    def stage_b(j, h, slot, masked):
        s = s_ref[slot]
        mt = mt_ref[slot]
        if masked:
            kchunk = lax.shift_right_arithmetic(j * tk + lax.broadcasted_iota(jnp.int32, (tk, 1), 0), CHUNK_SHIFT)
            qchunk = lax.shift_right_arithmetic(q0 + lax.broadcasted_iota(jnp.int32, (1, tq), 1), CHUNK_SHIFT)
            s = jnp.where(kchunk <= qchunk, s, NEG_INF)
            mt = jnp.max(s, axis=0, keepdims=True)
        m_old = m_ref[h]
        m_new = jnp.maximum(m_old, mt)
        alpha = jnp.exp2(m_old - m_new)
        m_ref[h] = m_new
        alpha_ref[h] = alpha
        p_ref[slot] = jnp.exp2(s - m_new).astype(BF16)

    def stage_c(j, h, slot):
        pv = jnp.dot(cT_ref[0, j], p_ref[slot], preferred_element_type=F32)
        acc_ref[h] = alpha_ref[h] * acc_ref[h] + pv

    last = N_HEADS - 1
    p_ref[last % 2] = jnp.zeros((tk, tq), BF16)
    alpha_ref[last] = jnp.ones((1, tq), F32)

    def tile_body(j, masked):
        j_next = jnp.minimum(j + 1, n_tiles - 1)
        j_prev = jnp.maximum(j - 1, 0)
        for h in range(N_HEADS):
            if h < last:
                stage_a(j, h + 1, (h + 1) % 2)
            else:
                stage_a(j_next, 0, 0)
            stage_b(j, h, h % 2, masked)
            if h > 0:
                stage_c(j, h - 1, (h - 1) % 2)
            else:
                stage_c(j_prev, last, last % 2)

    def full_body(j, carry):
        tile_body(j, False)
        return carry

    def masked_body(j, carry):
        tile_body(j, True)
        return carry

    stage_a(0, 0, 0)
    lax.fori_loop(0, n_full, full_body, 0)
    stage_a(jnp.minimum(n_full, n_tiles - 1), 0, 0)
    lax.fori_loop(n_full, n_all, masked_body, 0)
    stage_c(n_all - 1, last, last % 2)

    for h in range(N_HEADS):
        att_t = (acc_ref[h, 0:KV_LORA, :] / acc_ref[h, KV_LORA:KV_LORA + 1, :]).astype(BF16)
        o_t = jnp.dot(wuvT_ref[h], att_t, preferred_element_type=F32)
        oT_ref[0, h * V_HEAD:(h + 1) * V_HEAD, :] = o_t.astype(BF16)


def _attention(qT, kcat, cT, wuvT, *, tq, tk, q_offset, n_keys):
    bsz, _, _, t_q = qT.shape
    t_k = kcat.shape[1]
    n_tiles = t_k // tk
    kern = functools.partial(_attention_kernel, tq=tq, tk=tk, q_offset=q_offset, n_keys=n_keys,
                             n_tiles=n_tiles)
    return pl.pallas_call(
        kern,
        grid=(bsz, t_q // tq),
        in_specs=[
            pl.BlockSpec((1, N_HEADS, QK_CAT, tq), lambda b, i: (b, 0, 0, i)),
            pl.BlockSpec((1, t_k, QK_CAT), lambda b, i: (b, 0, 0)),
            pl.BlockSpec((1, n_tiles, V_ROWS, tk), lambda b, i: (b, 0, 0, 0)),
            _const_spec((N_HEADS, V_HEAD, KV_LORA)),
        ],
        out_specs=pl.BlockSpec((1, N_HEADS * V_HEAD, tq), lambda b, i: (b, 0, i)),
        out_shape=jax.ShapeDtypeStruct((bsz, N_HEADS * V_HEAD, t_q), BF16),
        scratch_shapes=[
            pltpu.VMEM((N_HEADS, 1, tq), F32),
            pltpu.VMEM((N_HEADS, 1, tq), F32),
            pltpu.VMEM((N_HEADS, V_ROWS, tq), F32),
            pltpu.VMEM((2, tk, tq), F32),
            pltpu.VMEM((2, 1, tq), F32),
            pltpu.VMEM((2, tk, tq), BF16),
        ],
        compiler_params=_params("parallel", "arbitrary"),
        name="attention",
    )(qT, kcat, cT, wuvT)


def _layer_norm(r, g, b):
    mu = jnp.mean(r, axis=-1, keepdims=True)
    cen = r - mu
    var = jnp.mean(cen * cen, axis=-1, keepdims=True)
    return cen * lax.rsqrt(var + LN_EPS) * g + b


def _mix_kernel(x_ref, oT_ref, u_ref, pre_ref, wg_ref, bg_ref, wup_ref, wpool_ref, pscale_ref, wo_ref,
                g1_ref, b1_ref, h_ref, ext_ref, d_ref, *, seg_len, n_seg, pos0_fn, alpha):
    d_model = x_ref.shape[-1]
    x = x_ref[...]
    xb = x.astype(BF16)
    zg = jnp.dot(xb, wg_ref[...], preferred_element_type=F32) + bg_ref[...]
    gates = 1.0 / (1.0 + jnp.exp(-zg))

    pos0 = pos0_fn(pl.program_id(0))
    row = lax.broadcasted_iota(jnp.int32, (seg_len, 1), 0)
    for s in range(n_seg):
        ext_ref[0:POOL_HALO, :] = pre_ref[s]
        ext_ref[POOL_HALO:POOL_HALO + seg_len, :] = u_ref[s * seg_len:(s + 1) * seg_len, :]
        for g, w in enumerate(POOL_WINDOWS):
            lanes = slice(g * POOL_GROUP_DIM, (g + 1) * POOL_GROUP_DIM)
            cur = ext_ref[POOL_HALO:POOL_HALO + seg_len, lanes]
            acc = cur
            for k in range(1, w):
                acc = acc + ext_ref[POOL_HALO - k:POOL_HALO - k + seg_len, lanes]
            cnt = jnp.minimum(pos0 + row + 1, w).astype(F32)
            d_ref[s * seg_len:(s + 1) * seg_len, lanes] = (acc * (1.0 / cnt) - cur).astype(BF16)

    pool_out = d_model // POOL_GROUPS
    parts = []
    for g in range(POOL_GROUPS):
        dg = d_ref[:, g * POOL_GROUP_DIM:(g + 1) * POOL_GROUP_DIM]
        parts.append(jnp.dot(dg, wpool_ref[g], preferred_element_type=F32))
    p_branch = jnp.concatenate(parts, axis=-1) * pscale_ref[...]

    a_branch = lax.dot_general(oT_ref[...], wup_ref[...], (((0,), (0,)), ((), ())),
                               preferred_element_type=F32)
    m = gates[:, :d_model] * a_branch + gates[:, d_model:] * p_branch
    r = alpha * x + jnp.dot(m.astype(BF16), wo_ref[...], preferred_element_type=F32)
    h_ref[...] = _layer_norm(r, g1_ref[...], b1_ref[...])


def _mix(x2, oT, u2, prefix, wg, bg, wup, wpool, pscale, wo, g1, b1, *, seg_len, pos0_fn, alpha):
    n, d = x2.shape
    tm = TOKEN_TILE
    n_seg = tm // seg_len
    _, hv, t_o = oT.shape
    tiles_per_row = t_o // tm
    kern = functools.partial(_mix_kernel, seg_len=seg_len, n_seg=n_seg, pos0_fn=pos0_fn, alpha=alpha)
    return pl.pallas_call(
        kern,
        grid=(n // tm,),
        in_specs=[
            pl.BlockSpec((tm, d), lambda i: (i, 0)),
            pl.BlockSpec((None, hv, tm), lambda i: (i // tiles_per_row, 0, i % tiles_per_row)),
            pl.BlockSpec((tm, POOL_WIDTH), lambda i: (i, 0)),
            pl.BlockSpec((n_seg, POOL_HALO, POOL_WIDTH), lambda i: (i, 0, 0)),
            _const_spec(wg.shape), _const_spec(bg.shape), _const_spec(wup.shape),
            _const_spec(wpool.shape), _const_spec(pscale.shape), _const_spec(wo.shape),
            _const_spec(g1.shape), _const_spec(b1.shape),
        ],
        out_specs=pl.BlockSpec((tm, d), lambda i: (i, 0)),
        out_shape=jax.ShapeDtypeStruct((n, d), F32),
        scratch_shapes=[
            pltpu.VMEM((POOL_HALO + seg_len, POOL_WIDTH), F32),
            pltpu.VMEM((tm, POOL_WIDTH), BF16),
        ],
        compiler_params=_params("parallel"),
        name="mix",
    )(x2, oT, u2, prefix, wg, bg, wup, wpool, pscale, wo, g1, b1)


def _ffn_kernel(h_ref, wgate_ref, wupf_ref, wdown_ref, g2_ref, b2_ref, y_ref, *, alpha):
    h = h_ref[...]
    hb = h.astype(BF16)
    f = None
    lo = 0
    for width in FF_CHUNKS:
        gate = jnp.dot(hb, wgate_ref[:, lo:lo + width], preferred_element_type=F32)
        up = jnp.dot(hb, wupf_ref[:, lo:lo + width], preferred_element_type=F32)
        act = (gate / (1.0 + jnp.exp(-gate)) * up).astype(BF16)
        part = jnp.dot(act, wdown_ref[lo:lo + width, :], preferred_element_type=F32)
        f = part if f is None else f + part
        lo += width
    y_ref[...] = _layer_norm(alpha * h + f, g2_ref[...], b2_ref[...])


def _ffn(h2, wgate, wupf, wdown, g2, b2, *, alpha):
    n, d = h2.shape
    tm = TOKEN_TILE
    assert sum(FF_CHUNKS) == wgate.shape[1]
    return pl.pallas_call(
        functools.partial(_ffn_kernel, alpha=alpha),
        grid=(n // tm,),
        in_specs=[
            pl.BlockSpec((tm, d), lambda i: (i, 0)),
            _const_spec(wgate.shape), _const_spec(wupf.shape), _const_spec(wdown.shape),
            _const_spec(g2.shape), _const_spec(b2.shape),
        ],
        out_specs=pl.BlockSpec((tm, d), lambda i: (i, 0)),
        out_shape=jax.ShapeDtypeStruct((n, d), F32),
        compiler_params=_params("parallel"),
        name="ffn",
    )(h2, wgate, wupf, wdown, g2, b2)


def _prompt_pos0(tiles_per_seq, i):
    return (i % tiles_per_seq) * TOKEN_TILE


def _sample_pos0(past, i):
    return past


def kernel(x_prompt, x_sample, cache_kv_latent, cache_k_rope, state_pool, w_in, b_gate, q_norm_g, w_uq,
           kv_norm_g, w_uk, w_uv, w_attn_up, w_pool, pool_scale, w_o, ln1_g, ln1_b, w_gate, w_up, w_down,
           ln2_g, ln2_b):
    depth = w_in.shape[0]
    bp, n_p, d = x_prompt.shape
    bs, n_s, _ = x_sample.shape
    past = cache_kv_latent.shape[2]
    alpha = (2 * depth) ** 0.25
    half = QK_ROPE // 2
    o1 = Q_LORA
    o2 = o1 + KV_LORA
    o3 = o2 + QK_ROPE
    o4 = o3 + POOL_WIDTH

    pos_p = jnp.arange(n_p, dtype=jnp.int32)
    pos_s = past + (jnp.arange(bs * n_s, dtype=jnp.int32) % n_s)
    n_keys_s = past + n_s
    keys_pad_s = -(-n_keys_s // 128) * 128

    h_p, h_s = x_prompt, x_sample
    c_p_list, kr_p_list, pool_p_list = [], [], []
    c_s_list, kr_s_list, pool_s_list = [], [], []
    for l in range(depth):
        wl = w_in[l]
        w_kr = wl[:, o2:o3]
        w_kr_rot = jnp.concatenate([-w_kr[:, half:], w_kr[:, :half]], axis=1)
        w1 = jnp.concatenate([wl[:, :o2], wl[:, o3:o4], w_kr, w_kr_rot], axis=1).astype(BF16)
        wg = wl[:, o4:].astype(BF16)
        bg = b_gate[l][None, :]
        wq = _fold_q(w_uq[l], w_uk[l])
        qg = q_norm_g[l][None, :]
        kvg = kv_norm_g[l][None, :]
        wuvT = jnp.transpose(w_uv[l], (1, 2, 0)).astype(BF16)
        wup = w_attn_up[l].astype(BF16)
        wpool = w_pool[l].astype(BF16)
        pscale = pool_scale[l][None, :]
        wo = w_o[l].astype(BF16)
        g1, b1 = ln1_g[l][None, :], ln1_b[l][None, :]
        g2, b2 = ln2_g[l][None, :], ln2_b[l][None, :]
        wgate, wupf, wdown = w_gate[l].astype(BF16), w_up[l].astype(BF16), w_down[l].astype(BF16)

        qT, kcat, cT, c, kr, u = _mixer_in(h_p, pos_p, w1, wq, qg, kvg)
        oT = _attention(qT, kcat, cT, wuvT, tq=Q_TILE, tk=KV_TILE, q_offset=0, n_keys=n_p)
        tiles_per_seq = n_p // TOKEN_TILE
        tails = u.reshape(bp, tiles_per_seq, TOKEN_TILE, POOL_WIDTH)[:, :, TOKEN_TILE - POOL_HALO:, :]
        prefix = jnp.concatenate([jnp.zeros_like(tails[:, :1]), tails[:, :-1]], axis=1)
        prefix = prefix.reshape(bp * tiles_per_seq, POOL_HALO, POOL_WIDTH)
        hmid = _mix(h_p.reshape(bp * n_p, d), oT, u.reshape(bp * n_p, POOL_WIDTH), prefix,
                    wg, bg, wup, wpool, pscale, wo, g1, b1, seg_len=TOKEN_TILE,
                    pos0_fn=functools.partial(_prompt_pos0, tiles_per_seq), alpha=alpha)
        y = _ffn(hmid, wgate, wupf, wdown, g2, b2, alpha=alpha)
        c_p_list.append(c)
        kr_p_list.append(kr)
        pool_p_list.append(u[:, n_p - POOL_STATE:])
        h_p = y.reshape(bp, n_p, d)

        xs = h_s.reshape(1, bs * n_s, d)
        qT, _, _, c, kr, u = _mixer_in(xs, pos_s, w1, wq, qg, kvg)
        c = c.reshape(bs, n_s, KV_LORA)
        kr = kr.reshape(bs, n_s, QK_ROPE)
        u = u.reshape(bs, n_s, POOL_WIDTH)
        qT = jnp.transpose(qT.reshape(N_HEADS, QK_CAT, bs, n_s), (2, 0, 1, 3))
        c_all = jnp.concatenate([cache_kv_latent[l], c], axis=1)
        kr_all = jnp.concatenate([cache_k_rope[l], kr], axis=1)
        kcat = jnp.concatenate([c_all, kr_all], axis=-1).astype(BF16)
        kcat = jnp.pad(kcat, ((0, 0), (0, keys_pad_s - n_keys_s), (0, 0)))
        cT = jnp.transpose(jnp.pad(c_all, ((0, 0), (0, keys_pad_s - n_keys_s), (0, 0))), (0, 2, 1))
        cT = jnp.concatenate([cT, jnp.ones((bs, 1, keys_pad_s), F32),
                              jnp.zeros((bs, V_ROWS - KV_LORA - 1, keys_pad_s), F32)], axis=1)
        cT = cT.astype(BF16)[:, None]
        oT = _attention(qT, kcat, cT, wuvT, tq=n_s, tk=keys_pad_s, q_offset=past, n_keys=n_keys_s)
        oT2 = jnp.transpose(oT, (1, 0, 2)).reshape(1, N_HEADS * V_HEAD, bs * n_s)
        prefix = jnp.pad(state_pool[l], ((0, 0), (POOL_HALO - POOL_STATE, 0), (0, 0)))
        hmid = _mix(h_s.reshape(bs * n_s, d), oT2, u.reshape(bs * n_s, POOL_WIDTH), prefix,
                    wg, bg, wup, wpool, pscale, wo, g1, b1, seg_len=n_s,
                    pos0_fn=functools.partial(_sample_pos0, past), alpha=alpha)
        y = _ffn(hmid, wgate, wupf, wdown, g2, b2, alpha=alpha)
        u_all = jnp.concatenate([state_pool[l], u], axis=1)
        c_s_list.append(c)
        kr_s_list.append(kr)
        pool_s_list.append(u_all[:, u_all.shape[1] - POOL_STATE:])
        h_s = y.reshape(bs, n_s, d)

    return (h_p, h_s, jnp.stack(c_p_list), jnp.stack(kr_p_list), jnp.stack(pool_p_list),
            jnp.stack(c_s_list), jnp.stack(kr_s_list), jnp.stack(pool_s_list))
```

```python
import functools
import math

import jax
import jax.numpy as jnp
from jax import lax
from jax.experimental import pallas as pl
from jax.experimental.pallas import tpu as pltpu

F32 = jnp.float32
BF16 = jnp.bfloat16

N_HEADS = 8
QK_NOPE = 64
QK_ROPE = 32
QK_HEAD = QK_NOPE + QK_ROPE
V_HEAD = 64
Q_LORA = 256
KV_LORA = 128
QK_CAT = KV_LORA + QK_ROPE
Q_ROWS = KV_LORA + 2 * QK_ROPE
V_ROWS = KV_LORA + 16
CHUNK = 64
CHUNK_SHIFT = 6
ROPE_THETA = 10000.0
ATTN_SCALE = QK_HEAD ** -0.5
POOL_WIDTH = 512
POOL_GROUPS = 4
POOL_GROUP_DIM = POOL_WIDTH // POOL_GROUPS
POOL_WINDOWS = (2, 4, 8, 16)
POOL_STATE = max(POOL_WINDOWS) - 1
POOL_HALO = POOL_STATE + 1
RMS_EPS = 1e-6
LN_EPS = 1e-5
NEG_INF = -1e30
L_MIN = 2.0 ** -60
Q_SCALE = ATTN_SCALE * math.log2(math.e)

VMEM_LIMIT_BYTES = 56 * 1024 * 1024

TOKEN_TILE = 512
KV_TILE = 512
Q_TILE = 512
FF_CHUNKS = (768, 768, 768, 512)


def _const_spec(shape):
    nd = len(shape)
    return pl.BlockSpec(shape, lambda *_: (0,) * nd, pipeline_mode=pl.Buffered(1))


def _params(*sem):
    return pltpu.CompilerParams(dimension_semantics=sem, vmem_limit_bytes=VMEM_LIMIT_BYTES)


def _fold_q_kernel(wuk_ref, wuqn_ref, wuqr_ref, out_ref):
    half = QK_ROPE // 2
    for h in range(N_HEADS):
        lat = jnp.dot(wuk_ref[h], wuqn_ref[h], preferred_element_type=F32,
                      precision=lax.Precision.HIGHEST)
        rope = wuqr_ref[h]
        out_ref[h, 0:KV_LORA, :] = lat.astype(BF16)
        out_ref[h, KV_LORA:KV_LORA + QK_ROPE, :] = rope.astype(BF16)
        out_ref[h, KV_LORA + QK_ROPE:KV_LORA + QK_ROPE + half, :] = (-rope[half:]).astype(BF16)
        out_ref[h, KV_LORA + QK_ROPE + half:Q_ROWS, :] = rope[:half].astype(BF16)


def _fold_q(w_uq, w_uk):
    wq = w_uq.reshape(Q_LORA, N_HEADS, QK_HEAD)
    wuqn_t = jnp.transpose(wq[:, :, :QK_NOPE], (1, 2, 0))
    wuqr_t = jnp.transpose(wq[:, :, QK_NOPE:], (1, 2, 0))
    wuk_t = jnp.transpose(w_uk, (1, 0, 2))
    out = pl.pallas_call(
        _fold_q_kernel,
        out_shape=jax.ShapeDtypeStruct((N_HEADS, Q_ROWS, Q_LORA), BF16),
        name="fold_q",
    )(wuk_t, wuqn_t, wuqr_t)
    return out.reshape(N_HEADS * Q_ROWS, Q_LORA)


def _mixer_in_kernel(x_ref, w1_ref, wq_ref, qg_ref, kvg_ref, cosr_ref, sinr_ref, cost_ref, sint_ref,
                     qT_ref, kcat_ref, cT_ref, ckv_ref, kr_ref, u_ref, qn_ref, ksq_ref):
    xb = x_ref[...].astype(BF16)
    z = jnp.dot(xb, w1_ref[...], preferred_element_type=F32)
    o1 = Q_LORA
    o2 = o1 + KV_LORA
    o3 = o2 + POOL_WIDTH
    o4 = o3 + QK_ROPE
    zq, zkv, u = z[:, :o1], z[:, o1:o2], z[:, o2:o3]
    zkr, zkr_rot = z[:, o3:o4], z[:, o4:o4 + QK_ROPE]
    u_ref[0] = u

    c = zkv * lax.rsqrt(jnp.mean(zkv * zkv, axis=-1, keepdims=True) + RMS_EPS) * kvg_ref[...]
    ckv_ref[0] = c
    kr = zkr * cosr_ref[...] + zkr_rot * sinr_ref[...]
    kr_ref[0] = kr
    cb = c.astype(BF16)
    krb = kr.astype(BF16)
    kcat_ref[0, :, 0:KV_LORA] = cb
    kcat_ref[0, :, KV_LORA:QK_CAT] = krb
    cf = cb.astype(F32)
    krf = krb.astype(F32)
    k_sq = jnp.sum(cf * cf, axis=-1, keepdims=True) + jnp.sum(krf * krf, axis=-1, keepdims=True)
    ksq_ref[0, 0] = jnp.broadcast_to(jnp.max(k_sq, axis=0, keepdims=True), ksq_ref.shape[2:])
    cT_ref[0, 0, 0:KV_LORA, :] = c.T.astype(BF16)
    ones_row = lax.broadcasted_iota(jnp.int32, (V_ROWS - KV_LORA, c.shape[0]), 0) == 0
    cT_ref[0, 0, KV_LORA:V_ROWS, :] = jnp.where(ones_row, 1.0, 0.0).astype(BF16)

    zqn = zq * lax.rsqrt(jnp.mean(zq * zq, axis=-1, keepdims=True) + RMS_EPS) * (qg_ref[...] * Q_SCALE)
    zqn_t = zqn.T.astype(BF16)
    q_all = jnp.dot(wq_ref[...], zqn_t, preferred_element_type=F32)
    cos_t = cost_ref[...]
    sin_t = sint_ref[...]
    for h in range(N_HEADS):
        b = h * Q_ROWS
        lat = q_all[b:b + KV_LORA].astype(BF16)
        rope = (q_all[b + KV_LORA:b + KV_LORA + QK_ROPE] * cos_t
                + q_all[b + KV_LORA + QK_ROPE:b + Q_ROWS] * sin_t).astype(BF16)
        qT_ref[0, h, 0:KV_LORA, :] = lat
        qT_ref[0, h, KV_LORA:QK_CAT, :] = rope
        latf = lat.astype(F32)
        ropef = rope.astype(F32)
        q_sq = jnp.sum(latf * latf, axis=0, keepdims=True) + jnp.sum(ropef * ropef, axis=0, keepdims=True)
        qn_ref[0, h:h + 1, :] = jnp.sqrt(q_sq)


def _mixer_in(x, pos, w1, wq, qg, kvg):
    bsz, t, d = x.shape
    tm = TOKEN_TILE
    nt = t // tm
    half = QK_ROPE // 2
    inv = 1.0 / (ROPE_THETA ** (jnp.arange(half, dtype=F32) / half))
    ang = pos.astype(F32)[:, None] * inv[None, :]
    cos_r = jnp.tile(jnp.cos(ang), (1, 2))
    sin_r = jnp.tile(jnp.sin(ang), (1, 2))
    cos_t, sin_t = cos_r.T, sin_r.T
    n1 = w1.shape[1]
    outs = pl.pallas_call(
        _mixer_in_kernel,
        grid=(bsz, nt),
        in_specs=[
            pl.BlockSpec((None, tm, d), lambda b, i: (b, i, 0)),
            _const_spec((d, n1)),
            _const_spec((N_HEADS * Q_ROWS, Q_LORA)),
            _const_spec((1, Q_LORA)),
            _const_spec((1, KV_LORA)),
            pl.BlockSpec((tm, QK_ROPE), lambda b, i: (i, 0)),
            pl.BlockSpec((tm, QK_ROPE), lambda b, i: (i, 0)),
            pl.BlockSpec((QK_ROPE, tm), lambda b, i: (0, i)),
            pl.BlockSpec((QK_ROPE, tm), lambda b, i: (0, i)),
        ],
        out_specs=[
            pl.BlockSpec((1, N_HEADS, QK_CAT, tm), lambda b, i: (b, 0, 0, i)),
            pl.BlockSpec((1, tm, QK_CAT), lambda b, i: (b, i, 0)),
            pl.BlockSpec((1, 1, V_ROWS, tm), lambda b, i: (b, i, 0, 0)),
            pl.BlockSpec((1, tm, KV_LORA), lambda b, i: (b, i, 0)),
            pl.BlockSpec((1, tm, QK_ROPE), lambda b, i: (b, i, 0)),
            pl.BlockSpec((1, tm, POOL_WIDTH), lambda b, i: (b, i, 0)),
            pl.BlockSpec((1, N_HEADS, tm), lambda b, i: (b, 0, i)),
            pl.BlockSpec((1, 1, 8, 128), lambda b, i: (b, i, 0, 0)),
        ],
        out_shape=[
            jax.ShapeDtypeStruct((bsz, N_HEADS, QK_CAT, t), BF16),
            jax.ShapeDtypeStruct((bsz, t, QK_CAT), BF16),
            jax.ShapeDtypeStruct((bsz, nt, V_ROWS, tm), BF16),
            jax.ShapeDtypeStruct((bsz, t, KV_LORA), F32),
            jax.ShapeDtypeStruct((bsz, t, QK_ROPE), F32),
            jax.ShapeDtypeStruct((bsz, t, POOL_WIDTH), F32),
            jax.ShapeDtypeStruct((bsz, N_HEADS, t), F32),
            jax.ShapeDtypeStruct((bsz, nt, 8, 128), F32),
        ],
        compiler_params=_params("parallel", "parallel"),
        name="mixer_in",
    )(x, w1, wq, qg, kvg, cos_r, sin_r, cos_t, sin_t)
    return outs


def _attention_kernel(kmax_ref, qT_ref, qn_ref, kcat_ref, cT_ref, wuvT_ref, oT_ref, m_ref, alpha_ref, acc_ref,
                      s_ref, mt_ref, p_ref, *, tq, tk, q_offset, n_keys, n_tiles, bounded):
    i = pl.program_id(1)
    q0 = q_offset + i * tq
    first_end = (q0 // CHUNK + 1) * CHUNK
    last_end = ((q0 + tq - 1) // CHUNK + 1) * CHUNK
    n_all = (jnp.minimum(last_end, n_keys) + tk - 1) // tk
    n_full = jnp.minimum(first_end // tk, n_all)
    last = N_HEADS - 1

    def scores(j, h):
        k = kcat_ref[0, pl.ds(pl.multiple_of(j * tk, tk), tk), :]
        return jnp.dot(k, qT_ref[0, h], preferred_element_type=F32)

    def visible(j):
        kchunk = lax.shift_right_arithmetic(j * tk + lax.broadcasted_iota(jnp.int32, (tk, 1), 0), CHUNK_SHIFT)
        qchunk = lax.shift_right_arithmetic(q0 + lax.broadcasted_iota(jnp.int32, (1, tq), 1), CHUNK_SHIFT)
        return kchunk <= qchunk

    def run_tiles(first_stage, tile_body, final_stage):
        first_stage(0, False)
        lax.fori_loop(0, n_full, lambda j, c: tile_body(j, False) or c, 0)
        first_stage(jnp.minimum(n_full, n_tiles - 1), True)
        lax.fori_loop(n_full, n_all, lambda j, c: tile_body(j, True) or c, 0)
        if final_stage is not None:
            final_stage(n_all - 1)

    def stage_p(j, h, slot, masked):
        x = scores(j, h) - qn_ref[0, h:h + 1, :] * kmax_ref[pl.program_id(0)]
        if masked:
            x = jnp.where(visible(j), x, NEG_INF)
        p_ref[slot] = jnp.exp2(x).astype(BF16)

    def stage_v(j, h, slot):
        acc_ref[h] += jnp.dot(cT_ref[0, j], p_ref[slot], preferred_element_type=F32)

    def bounded_tile(j, masked):
        j_next = jnp.minimum(j + 1, n_tiles - 1)
        for h in range(N_HEADS):
            if h < last:
                stage_p(j, h + 1, (h + 1) % 2, masked)
            else:
                stage_p(j_next, 0, 0, masked)
            stage_v(j, h, h % 2)

    def bounded_path():
        acc_ref[...] = jnp.zeros(acc_ref.shape, F32)
        run_tiles(lambda j, masked: stage_p(j, 0, 0, masked), bounded_tile, None)

    def stage_a(j, h, slot):
        s = scores(j, h)
        s_ref[slot] = s
        mt_ref[slot] = jnp.max(s, axis=0, keepdims=True)

    def stage_b(j, h, slot, masked):
        s = s_ref[slot]
        mt = mt_ref[slot]
        if masked:
            s = jnp.where(visible(j), s, NEG_INF)
            mt = jnp.max(s, axis=0, keepdims=True)
        m_old = m_ref[h]
        m_new = jnp.maximum(m_old, mt)
        m_ref[h] = m_new
        alpha_ref[h] = jnp.exp2(m_old - m_new)
        p_ref[slot] = jnp.exp2(s - m_new).astype(BF16)

    def stage_c(j, h, slot):
        pv = jnp.dot(cT_ref[0, j], p_ref[slot], preferred_element_type=F32)
        acc_ref[h] = alpha_ref[h] * acc_ref[h] + pv

    def exact_tile(j, masked):
        j_next = jnp.minimum(j + 1, n_tiles - 1)
        j_prev = jnp.maximum(j - 1, 0)
        for h in range(N_HEADS):
            if h < last:
                stage_a(j, h + 1, (h + 1) % 2)
            else:
                stage_a(j_next, 0, 0)
            stage_b(j, h, h % 2, masked)
            if h > 0:
                stage_c(j, h - 1, (h - 1) % 2)
            else:
                stage_c(j_prev, last, last % 2)

    def exact_path():
        m_ref[...] = jnp.full(m_ref.shape, NEG_INF, F32)
        acc_ref[...] = jnp.zeros(acc_ref.shape, F32)
        p_ref[last % 2] = jnp.zeros((tk, tq), BF16)
        alpha_ref[last] = jnp.ones((1, tq), F32)
        run_tiles(lambda j, masked: stage_a(j, 0, 0), exact_tile,
                  lambda j: stage_c(j, last, last % 2))

    if bounded:
        bounded_path()
        denom = jnp.concatenate([acc_ref[h, KV_LORA:KV_LORA + 1, :] for h in range(N_HEADS)], axis=0)
        pl.when(jnp.logical_not(jnp.min(denom) >= L_MIN))(exact_path)
    else:
        exact_path()

    for h in range(N_HEADS):
        att_t = (acc_ref[h, 0:KV_LORA, :] / acc_ref[h, KV_LORA:KV_LORA + 1, :]).astype(BF16)
        o_t = jnp.dot(wuvT_ref[h], att_t, preferred_element_type=F32)
        oT_ref[0, h * V_HEAD:(h + 1) * V_HEAD, :] = o_t.astype(BF16)


def _attention(qT, qn, kmax, kcat, cT, wuvT, *, tq, tk, q_offset, n_keys, bounded):
    bsz, _, _, t_q = qT.shape
    t_k = kcat.shape[1]
    n_tiles = t_k // tk
    kern = functools.partial(_attention_kernel, tq=tq, tk=tk, q_offset=q_offset, n_keys=n_keys,
                             n_tiles=n_tiles, bounded=bounded)
    return pl.pallas_call(
        kern,
        grid=(bsz, t_q // tq),
        in_specs=[
            pl.BlockSpec(memory_space=pltpu.SMEM),
            pl.BlockSpec((1, N_HEADS, QK_CAT, tq), lambda b, i: (b, 0, 0, i)),
            pl.BlockSpec((1, N_HEADS, tq), lambda b, i: (b, 0, i)),
            pl.BlockSpec((1, t_k, QK_CAT), lambda b, i: (b, 0, 0)),
            pl.BlockSpec((1, n_tiles, V_ROWS, tk), lambda b, i: (b, 0, 0, 0)),
            _const_spec((N_HEADS, V_HEAD, KV_LORA)),
        ],
        out_specs=pl.BlockSpec((1, N_HEADS * V_HEAD, tq), lambda b, i: (b, 0, i)),
        out_shape=jax.ShapeDtypeStruct((bsz, N_HEADS * V_HEAD, t_q), BF16),
        scratch_shapes=[
            pltpu.VMEM((N_HEADS, 1, tq), F32),
            pltpu.VMEM((N_HEADS, 1, tq), F32),
            pltpu.VMEM((N_HEADS, V_ROWS, tq), F32),
            pltpu.VMEM((2, tk, tq), F32),
            pltpu.VMEM((2, 1, tq), F32),
            pltpu.VMEM((2, tk, tq), BF16),
        ],
        compiler_params=_params("parallel", "arbitrary"),
        name="attention",
    )(kmax, qT, qn, kcat, cT, wuvT)


def _layer_norm(r, g, b):
    mu = jnp.mean(r, axis=-1, keepdims=True)
    cen = r - mu
    var = jnp.mean(cen * cen, axis=-1, keepdims=True)
    return cen * lax.rsqrt(var + LN_EPS) * g + b


def _mix_kernel(x_ref, oT_ref, u_ref, pre_ref, wg_ref, bg_ref, wup_ref, wpool_ref, pscale_ref, wo_ref,
                g1_ref, b1_ref, h_ref, ext_ref, d_ref, *, seg_len, n_seg, pos0_fn, alpha):
    d_model = x_ref.shape[-1]
    x = x_ref[...]
    xb = x.astype(BF16)
    zg = jnp.dot(xb, wg_ref[...], preferred_element_type=F32) + bg_ref[...]
    gates = 1.0 / (1.0 + jnp.exp(-zg))

    pos0 = pos0_fn(pl.program_id(0))
    row = lax.broadcasted_iota(jnp.int32, (seg_len, 1), 0)
    for s in range(n_seg):
        ext_ref[0:POOL_HALO, :] = pre_ref[s]
        ext_ref[POOL_HALO:POOL_HALO + seg_len, :] = u_ref[s * seg_len:(s + 1) * seg_len, :]
        for g, w in enumerate(POOL_WINDOWS):
            lanes = slice(g * POOL_GROUP_DIM, (g + 1) * POOL_GROUP_DIM)
            cur = ext_ref[POOL_HALO:POOL_HALO + seg_len, lanes]
            acc = cur
            for k in range(1, w):
                acc = acc + ext_ref[POOL_HALO - k:POOL_HALO - k + seg_len, lanes]
            cnt = jnp.minimum(pos0 + row + 1, w).astype(F32)
            d_ref[s * seg_len:(s + 1) * seg_len, lanes] = (acc * (1.0 / cnt) - cur).astype(BF16)

    pool_out = d_model // POOL_GROUPS
    parts = []
    for g in range(POOL_GROUPS):
        dg = d_ref[:, g * POOL_GROUP_DIM:(g + 1) * POOL_GROUP_DIM]
        parts.append(jnp.dot(dg, wpool_ref[g], preferred_element_type=F32))
    p_branch = jnp.concatenate(parts, axis=-1) * pscale_ref[...]

    a_branch = lax.dot_general(oT_ref[...], wup_ref[...], (((0,), (0,)), ((), ())),
                               preferred_element_type=F32)
    m = gates[:, :d_model] * a_branch + gates[:, d_model:] * p_branch
    r = alpha * x + jnp.dot(m.astype(BF16), wo_ref[...], preferred_element_type=F32)
    h_ref[...] = _layer_norm(r, g1_ref[...], b1_ref[...])


def _mix(x2, oT, u2, prefix, wg, bg, wup, wpool, pscale, wo, g1, b1, *, seg_len, pos0_fn, alpha):
    n, d = x2.shape
    tm = TOKEN_TILE
    n_seg = tm // seg_len
    _, hv, t_o = oT.shape
    tiles_per_row = t_o // tm
    kern = functools.partial(_mix_kernel, seg_len=seg_len, n_seg=n_seg, pos0_fn=pos0_fn, alpha=alpha)
    return pl.pallas_call(
        kern,
        grid=(n // tm,),
        in_specs=[
            pl.BlockSpec((tm, d), lambda i: (i, 0)),
            pl.BlockSpec((None, hv, tm), lambda i: (i // tiles_per_row, 0, i % tiles_per_row)),
            pl.BlockSpec((tm, POOL_WIDTH), lambda i: (i, 0)),
            pl.BlockSpec((n_seg, POOL_HALO, POOL_WIDTH), lambda i: (i, 0, 0)),
            _const_spec(wg.shape), _const_spec(bg.shape), _const_spec(wup.shape),
            _const_spec(wpool.shape), _const_spec(pscale.shape), _const_spec(wo.shape),
            _const_spec(g1.shape), _const_spec(b1.shape),
        ],
        out_specs=pl.BlockSpec((tm, d), lambda i: (i, 0)),
        out_shape=jax.ShapeDtypeStruct((n, d), F32),
        scratch_shapes=[
            pltpu.VMEM((POOL_HALO + seg_len, POOL_WIDTH), F32),
            pltpu.VMEM((tm, POOL_WIDTH), BF16),
        ],
        compiler_params=_params("parallel"),
        name="mix",
    )(x2, oT, u2, prefix, wg, bg, wup, wpool, pscale, wo, g1, b1)


def _ffn_kernel(h_ref, wgate_ref, wupf_ref, wdown_ref, g2_ref, b2_ref, y_ref, *, alpha):
    h = h_ref[...]
    hb = h.astype(BF16)
    f = None
    lo = 0
    for width in FF_CHUNKS:
        gate = jnp.dot(hb, wgate_ref[:, lo:lo + width], preferred_element_type=F32)
        up = jnp.dot(hb, wupf_ref[:, lo:lo + width], preferred_element_type=F32)
        act = (gate / (1.0 + jnp.exp(-gate)) * up).astype(BF16)
        part = jnp.dot(act, wdown_ref[lo:lo + width, :], preferred_element_type=F32)
        f = part if f is None else f + part
        lo += width
    y_ref[...] = _layer_norm(alpha * h + f, g2_ref[...], b2_ref[...])


def _ffn(h2, wgate, wupf, wdown, g2, b2, *, alpha):
    n, d = h2.shape
    tm = TOKEN_TILE
    assert sum(FF_CHUNKS) == wgate.shape[1]
    return pl.pallas_call(
        functools.partial(_ffn_kernel, alpha=alpha),
        grid=(n // tm,),
        in_specs=[
            pl.BlockSpec((tm, d), lambda i: (i, 0)),
            _const_spec(wgate.shape), _const_spec(wupf.shape), _const_spec(wdown.shape),
            _const_spec(g2.shape), _const_spec(b2.shape),
        ],
        out_specs=pl.BlockSpec((tm, d), lambda i: (i, 0)),
        out_shape=jax.ShapeDtypeStruct((n, d), F32),
        compiler_params=_params("parallel"),
        name="ffn",
    )(h2, wgate, wupf, wdown, g2, b2)


def _prompt_pos0(tiles_per_seq, i):
    return (i % tiles_per_seq) * TOKEN_TILE


def _sample_pos0(past, i):
    return past


def kernel(x_prompt, x_sample, cache_kv_latent, cache_k_rope, state_pool, w_in, b_gate, q_norm_g, w_uq,
           kv_norm_g, w_uk, w_uv, w_attn_up, w_pool, pool_scale, w_o, ln1_g, ln1_b, w_gate, w_up, w_down,
           ln2_g, ln2_b):
    depth = w_in.shape[0]
    bp, n_p, d = x_prompt.shape
    bs, n_s, _ = x_sample.shape
    past = cache_kv_latent.shape[2]
    alpha = (2 * depth) ** 0.25
    half = QK_ROPE // 2
    o1 = Q_LORA
    o2 = o1 + KV_LORA
    o3 = o2 + QK_ROPE
    o4 = o3 + POOL_WIDTH

    pos_p = jnp.arange(n_p, dtype=jnp.int32)
    pos_s = past + (jnp.arange(bs * n_s, dtype=jnp.int32) % n_s)
    n_keys_s = past + n_s
    keys_pad_s = -(-n_keys_s // 128) * 128

    h_p, h_s = x_prompt, x_sample
    c_p_list, kr_p_list, pool_p_list = [], [], []
    c_s_list, kr_s_list, pool_s_list = [], [], []
    for l in range(depth):
        wl = w_in[l]
        w_kr = wl[:, o2:o3]
        w_kr_rot = jnp.concatenate([-w_kr[:, half:], w_kr[:, :half]], axis=1)
        w1 = jnp.concatenate([wl[:, :o2], wl[:, o3:o4], w_kr, w_kr_rot], axis=1).astype(BF16)
        wg = wl[:, o4:].astype(BF16)
        bg = b_gate[l][None, :]
        wq = _fold_q(w_uq[l], w_uk[l])
        qg = q_norm_g[l][None, :]
        kvg = kv_norm_g[l][None, :]
        wuvT = jnp.transpose(w_uv[l], (1, 2, 0)).astype(BF16)
        wup = w_attn_up[l].astype(BF16)
        wpool = w_pool[l].astype(BF16)
        pscale = pool_scale[l][None, :]
        wo = w_o[l].astype(BF16)
        g1, b1 = ln1_g[l][None, :], ln1_b[l][None, :]
        g2, b2 = ln2_g[l][None, :], ln2_b[l][None, :]
        wgate, wupf, wdown = w_gate[l].astype(BF16), w_up[l].astype(BF16), w_down[l].astype(BF16)

        qT, kcat, cT, c, kr, u, qn, ksq = _mixer_in(h_p, pos_p, w1, wq, qg, kvg)
        kmax = jnp.sqrt(jnp.max(ksq[:, :, 0, 0], axis=1))
        oT = _attention(qT, qn, kmax, kcat, cT, wuvT, tq=Q_TILE, tk=KV_TILE, q_offset=0, n_keys=n_p,
                        bounded=True)
        tiles_per_seq = n_p // TOKEN_TILE
        tails = u.reshape(bp, tiles_per_seq, TOKEN_TILE, POOL_WIDTH)[:, :, TOKEN_TILE - POOL_HALO:, :]
        prefix = jnp.concatenate([jnp.zeros_like(tails[:, :1]), tails[:, :-1]], axis=1)
        prefix = prefix.reshape(bp * tiles_per_seq, POOL_HALO, POOL_WIDTH)
        hmid = _mix(h_p.reshape(bp * n_p, d), oT, u.reshape(bp * n_p, POOL_WIDTH), prefix,
                    wg, bg, wup, wpool, pscale, wo, g1, b1, seg_len=TOKEN_TILE,
                    pos0_fn=functools.partial(_prompt_pos0, tiles_per_seq), alpha=alpha)
        y = _ffn(hmid, wgate, wupf, wdown, g2, b2, alpha=alpha)
        c_p_list.append(c)
        kr_p_list.append(kr)
        pool_p_list.append(u[:, n_p - POOL_STATE:])
        h_p = y.reshape(bp, n_p, d)

        xs = h_s.reshape(1, bs * n_s, d)
        qT, _, _, c, kr, u, qn, _ = _mixer_in(xs, pos_s, w1, wq, qg, kvg)
        c = c.reshape(bs, n_s, KV_LORA)
        kr = kr.reshape(bs, n_s, QK_ROPE)
        u = u.reshape(bs, n_s, POOL_WIDTH)
        qT = jnp.transpose(qT.reshape(N_HEADS, QK_CAT, bs, n_s), (2, 0, 1, 3))
        c_all = jnp.concatenate([cache_kv_latent[l], c], axis=1)
        kr_all = jnp.concatenate([cache_k_rope[l], kr], axis=1)
        kcat = jnp.concatenate([c_all, kr_all], axis=-1).astype(BF16)
        kcat = jnp.pad(kcat, ((0, 0), (0, keys_pad_s - n_keys_s), (0, 0)))
        cT = jnp.transpose(jnp.pad(c_all, ((0, 0), (0, keys_pad_s - n_keys_s), (0, 0))), (0, 2, 1))
        cT = jnp.concatenate([cT, jnp.ones((bs, 1, keys_pad_s), F32),
                              jnp.zeros((bs, V_ROWS - KV_LORA - 1, keys_pad_s), F32)], axis=1)
        cT = cT.astype(BF16)[:, None]
        qn = jnp.transpose(qn.reshape(N_HEADS, bs, n_s), (1, 0, 2))
        oT = _attention(qT, qn, jnp.zeros((bs,), F32), kcat, cT, wuvT, tq=n_s, tk=keys_pad_s, q_offset=past,
                        n_keys=n_keys_s, bounded=False)
        oT2 = jnp.transpose(oT, (1, 0, 2)).reshape(1, N_HEADS * V_HEAD, bs * n_s)
        prefix = jnp.pad(state_pool[l], ((0, 0), (POOL_HALO - POOL_STATE, 0), (0, 0)))
        hmid = _mix(h_s.reshape(bs * n_s, d), oT2, u.reshape(bs * n_s, POOL_WIDTH), prefix,
                    wg, bg, wup, wpool, pscale, wo, g1, b1, seg_len=n_s,
                    pos0_fn=functools.partial(_sample_pos0, past), alpha=alpha)
        y = _ffn(hmid, wgate, wupf, wdown, g2, b2, alpha=alpha)
        u_all = jnp.concatenate([state_pool[l], u], axis=1)
        c_s_list.append(c)
        kr_s_list.append(kr)
        pool_s_list.append(u_all[:, u_all.shape[1] - POOL_STATE:])
        h_s = y.reshape(bs, n_s, d)

    return (h_p, h_s, jnp.stack(c_p_list), jnp.stack(kr_p_list), jnp.stack(pool_p_list),
            jnp.stack(c_s_list), jnp.stack(kr_s_list), jnp.stack(pool_s_list))
```

```python
import functools
import math

import jax
import jax.numpy as jnp
from jax import lax
from jax.experimental import pallas as pl
from jax.experimental.pallas import tpu as pltpu

F32 = jnp.float32
BF16 = jnp.bfloat16

N_HEADS = 8
QK_NOPE = 64
QK_ROPE = 32
QK_HEAD = QK_NOPE + QK_ROPE
V_HEAD = 64
Q_LORA = 256
KV_LORA = 128
QK_CAT = KV_LORA + QK_ROPE
Q_ROWS = KV_LORA + 2 * QK_ROPE
VH_ROWS = V_HEAD + 16
CHUNK = 64
CHUNK_SHIFT = 6
ROPE_THETA = 10000.0
ATTN_SCALE = QK_HEAD ** -0.5
POOL_WIDTH = 512
POOL_GROUPS = 4
POOL_GROUP_DIM = POOL_WIDTH // POOL_GROUPS
POOL_WINDOWS = (2, 4, 8, 16)
POOL_STATE = max(POOL_WINDOWS) - 1
POOL_HALO = POOL_STATE + 1
RMS_EPS = 1e-6
LN_EPS = 1e-5
NEG_INF = -1e30
L_MIN = 2.0 ** -60
Q_SCALE = ATTN_SCALE * math.log2(math.e)

VMEM_LIMIT_BYTES = 56 * 1024 * 1024

TOKEN_TILE = 512
KV_TILE = 512
Q_TILE = 512
FF_CHUNKS = (768, 768, 768, 512)


def _const_spec(shape):
    nd = len(shape)
    return pl.BlockSpec(shape, lambda *_: (0,) * nd, pipeline_mode=pl.Buffered(1))


def _params(*sem):
    return pltpu.CompilerParams(dimension_semantics=sem, vmem_limit_bytes=VMEM_LIMIT_BYTES)


def _fold_q_kernel(wuk_ref, wuqn_ref, wuqr_ref, out_ref):
    half = QK_ROPE // 2
    for h in range(N_HEADS):
        lat = jnp.dot(wuk_ref[h], wuqn_ref[h], preferred_element_type=F32,
                      precision=lax.Precision.HIGHEST)
        rope = wuqr_ref[h]
        out_ref[h, 0:KV_LORA, :] = lat.astype(BF16)
        out_ref[h, KV_LORA:KV_LORA + QK_ROPE, :] = rope.astype(BF16)
        out_ref[h, KV_LORA + QK_ROPE:KV_LORA + QK_ROPE + half, :] = (-rope[half:]).astype(BF16)
        out_ref[h, KV_LORA + QK_ROPE + half:Q_ROWS, :] = rope[:half].astype(BF16)


def _fold_q(w_uq, w_uk):
    wq = w_uq.reshape(Q_LORA, N_HEADS, QK_HEAD)
    wuqn_t = jnp.transpose(wq[:, :, :QK_NOPE], (1, 2, 0))
    wuqr_t = jnp.transpose(wq[:, :, QK_NOPE:], (1, 2, 0))
    wuk_t = jnp.transpose(w_uk, (1, 0, 2))
    out = pl.pallas_call(
        _fold_q_kernel,
        out_shape=jax.ShapeDtypeStruct((N_HEADS, Q_ROWS, Q_LORA), BF16),
        name="fold_q",
    )(wuk_t, wuqn_t, wuqr_t)
    return out.reshape(N_HEADS * Q_ROWS, Q_LORA)


def _mixer_in_kernel(x_ref, w1_ref, wq_ref, wuv_ref, qg_ref, kvg_ref, cosr_ref, sinr_ref, cost_ref, sint_ref,
                     qT_ref, kcat_ref, vT_ref, ckv_ref, kr_ref, u_ref, qn_ref, ksq_ref):
    xb = x_ref[...].astype(BF16)
    z = jnp.dot(xb, w1_ref[...], preferred_element_type=F32)
    o1 = Q_LORA
    o2 = o1 + KV_LORA
    o3 = o2 + POOL_WIDTH
    o4 = o3 + QK_ROPE
    zq, zkv, u = z[:, :o1], z[:, o1:o2], z[:, o2:o3]
    zkr, zkr_rot = z[:, o3:o4], z[:, o4:o4 + QK_ROPE]
    u_ref[0] = u

    c = zkv * lax.rsqrt(jnp.mean(zkv * zkv, axis=-1, keepdims=True) + RMS_EPS) * kvg_ref[...]
    ckv_ref[0] = c
    kr = zkr * cosr_ref[...] + zkr_rot * sinr_ref[...]
    kr_ref[0] = kr
    cb = c.astype(BF16)
    krb = kr.astype(BF16)
    kcat_ref[0, :, 0:KV_LORA] = cb
    kcat_ref[0, :, KV_LORA:QK_CAT] = krb
    cf = cb.astype(F32)
    krf = krb.astype(F32)
    k_sq = jnp.sum(cf * cf, axis=-1, keepdims=True) + jnp.sum(krf * krf, axis=-1, keepdims=True)
    ksq_ref[0, 0] = jnp.broadcast_to(jnp.max(k_sq, axis=0, keepdims=True), ksq_ref.shape[2:])
    v_t = jnp.dot(wuv_ref[...], c.T.astype(BF16), preferred_element_type=F32)
    ones_row = lax.broadcasted_iota(jnp.int32, (VH_ROWS - V_HEAD, c.shape[0]), 0) == 0
    ones_rows = jnp.where(ones_row, 1.0, 0.0).astype(BF16)
    for h in range(N_HEADS):
        vT_ref[0, 0, h, 0:V_HEAD, :] = v_t[h * V_HEAD:(h + 1) * V_HEAD].astype(BF16)
        vT_ref[0, 0, h, V_HEAD:VH_ROWS, :] = ones_rows

    zqn = zq * lax.rsqrt(jnp.mean(zq * zq, axis=-1, keepdims=True) + RMS_EPS) * (qg_ref[...] * Q_SCALE)
    zqn_t = zqn.T.astype(BF16)
    q_all = jnp.dot(wq_ref[...], zqn_t, preferred_element_type=F32)
    cos_t = cost_ref[...]
    sin_t = sint_ref[...]
    for h in range(N_HEADS):
        b = h * Q_ROWS
        lat = q_all[b:b + KV_LORA].astype(BF16)
        rope = (q_all[b + KV_LORA:b + KV_LORA + QK_ROPE] * cos_t
                + q_all[b + KV_LORA + QK_ROPE:b + Q_ROWS] * sin_t).astype(BF16)
        qT_ref[0, h, 0:KV_LORA, :] = lat
        qT_ref[0, h, KV_LORA:QK_CAT, :] = rope
        latf = lat.astype(F32)
        ropef = rope.astype(F32)
        q_sq = jnp.sum(latf * latf, axis=0, keepdims=True) + jnp.sum(ropef * ropef, axis=0, keepdims=True)
        qn_ref[0, h:h + 1, :] = jnp.sqrt(q_sq)


def _mixer_in(x, pos, w1, wq, wuv, qg, kvg):
    bsz, t, d = x.shape
    tm = TOKEN_TILE
    nt = t // tm
    half = QK_ROPE // 2
    inv = 1.0 / (ROPE_THETA ** (jnp.arange(half, dtype=F32) / half))
    ang = pos.astype(F32)[:, None] * inv[None, :]
    cos_r = jnp.tile(jnp.cos(ang), (1, 2))
    sin_r = jnp.tile(jnp.sin(ang), (1, 2))
    cos_t, sin_t = cos_r.T, sin_r.T
    n1 = w1.shape[1]
    outs = pl.pallas_call(
        _mixer_in_kernel,
        grid=(bsz, nt),
        in_specs=[
            pl.BlockSpec((None, tm, d), lambda b, i: (b, i, 0)),
            _const_spec((d, n1)),
            _const_spec((N_HEADS * Q_ROWS, Q_LORA)),
            _const_spec((N_HEADS * V_HEAD, KV_LORA)),
            _const_spec((1, Q_LORA)),
            _const_spec((1, KV_LORA)),
            pl.BlockSpec((tm, QK_ROPE), lambda b, i: (i, 0)),
            pl.BlockSpec((tm, QK_ROPE), lambda b, i: (i, 0)),
            pl.BlockSpec((QK_ROPE, tm), lambda b, i: (0, i)),
            pl.BlockSpec((QK_ROPE, tm), lambda b, i: (0, i)),
        ],
        out_specs=[
            pl.BlockSpec((1, N_HEADS, QK_CAT, tm), lambda b, i: (b, 0, 0, i)),
            pl.BlockSpec((1, tm, QK_CAT), lambda b, i: (b, i, 0)),
            pl.BlockSpec((1, 1, N_HEADS, VH_ROWS, tm), lambda b, i: (b, i, 0, 0, 0)),
            pl.BlockSpec((1, tm, KV_LORA), lambda b, i: (b, i, 0)),
            pl.BlockSpec((1, tm, QK_ROPE), lambda b, i: (b, i, 0)),
            pl.BlockSpec((1, tm, POOL_WIDTH), lambda b, i: (b, i, 0)),
            pl.BlockSpec((1, N_HEADS, tm), lambda b, i: (b, 0, i)),
            pl.BlockSpec((1, 1, 8, 128), lambda b, i: (b, i, 0, 0)),
        ],
        out_shape=[
            jax.ShapeDtypeStruct((bsz, N_HEADS, QK_CAT, t), BF16),
            jax.ShapeDtypeStruct((bsz, t, QK_CAT), BF16),
            jax.ShapeDtypeStruct((bsz, nt, N_HEADS, VH_ROWS, tm), BF16),
            jax.ShapeDtypeStruct((bsz, t, KV_LORA), F32),
            jax.ShapeDtypeStruct((bsz, t, QK_ROPE), F32),
            jax.ShapeDtypeStruct((bsz, t, POOL_WIDTH), F32),
            jax.ShapeDtypeStruct((bsz, N_HEADS, t), F32),
            jax.ShapeDtypeStruct((bsz, nt, 8, 128), F32),
        ],
        compiler_params=_params("parallel", "parallel"),
        name="mixer_in",
    )(x, w1, wq, wuv, qg, kvg, cos_r, sin_r, cos_t, sin_t)
    return outs


def _attention_kernel(kmax_ref, qT_ref, qn_ref, kcat_ref, vT_ref, oT_ref, m_ref, alpha_ref, acc_ref,
                      s_ref, mt_ref, p_ref, *, tq, tk, q_offset, n_keys, n_tiles):
    den = V_HEAD

    def values(j, h):
        return vT_ref[0, j, h]

    i = pl.program_id(1)
    q0 = q_offset + i * tq
    first_end = (q0 // CHUNK + 1) * CHUNK
    last_end = ((q0 + tq - 1) // CHUNK + 1) * CHUNK
    n_all = (jnp.minimum(last_end, n_keys) + tk - 1) // tk
    n_full = jnp.minimum(first_end // tk, n_all)
    last = N_HEADS - 1

    def scores(j, h):
        k = kcat_ref[0, pl.ds(pl.multiple_of(j * tk, tk), tk), :]
        return jnp.dot(k, qT_ref[0, h], preferred_element_type=F32)

    def visible(j):
        kchunk = lax.shift_right_arithmetic(j * tk + lax.broadcasted_iota(jnp.int32, (tk, 1), 0), CHUNK_SHIFT)
        qchunk = lax.shift_right_arithmetic(q0 + lax.broadcasted_iota(jnp.int32, (1, tq), 1), CHUNK_SHIFT)
        return kchunk <= qchunk

    def run_tiles(first_stage, tile_body, final_stage):
        first_stage(0, False)
        lax.fori_loop(0, n_full, lambda j, c: tile_body(j, False) or c, 0)
        first_stage(jnp.minimum(n_full, n_tiles - 1), True)
        lax.fori_loop(n_full, n_all, lambda j, c: tile_body(j, True) or c, 0)
        if final_stage is not None:
            final_stage(n_all - 1)

    def stage_p(j, h, slot, masked):
        x = scores(j, h) - qn_ref[0, h:h + 1, :] * kmax_ref[pl.program_id(0)]
        if masked:
            x = jnp.where(visible(j), x, NEG_INF)
        p_ref[slot] = jnp.exp2(x).astype(BF16)

    def stage_v(j, h, slot):
        acc_ref[h] += jnp.dot(values(j, h), p_ref[slot], preferred_element_type=F32)

    def bounded_tile(j, masked):
        j_next = jnp.minimum(j + 1, n_tiles - 1)
        for h in range(N_HEADS):
            if h < last:
                stage_p(j, h + 1, (h + 1) % 2, masked)
            else:
                stage_p(j_next, 0, 0, masked)
            stage_v(j, h, h % 2)

    def bounded_path():
        acc_ref[...] = jnp.zeros(acc_ref.shape, F32)
        run_tiles(lambda j, masked: stage_p(j, 0, 0, masked), bounded_tile, None)

    def stage_a(j, h, slot):
        s = scores(j, h)
        s_ref[slot] = s
        mt_ref[slot] = jnp.max(s, axis=0, keepdims=True)

    def stage_b(j, h, slot, masked):
        s = s_ref[slot]
        mt = mt_ref[slot]
        if masked:
            s = jnp.where(visible(j), s, NEG_INF)
            mt = jnp.max(s, axis=0, keepdims=True)
        m_old = m_ref[h]
        m_new = jnp.maximum(m_old, mt)
        m_ref[h] = m_new
        alpha_ref[h] = jnp.exp2(m_old - m_new)
        p_ref[slot] = jnp.exp2(s - m_new).astype(BF16)

    def stage_c(j, h, slot):
        pv = jnp.dot(values(j, h), p_ref[slot], preferred_element_type=F32)
        acc_ref[h] = alpha_ref[h] * acc_ref[h] + pv

    def exact_tile(j, masked):
        j_next = jnp.minimum(j + 1, n_tiles - 1)
        j_prev = jnp.maximum(j - 1, 0)
        for h in range(N_HEADS):
            if h < last:
                stage_a(j, h + 1, (h + 1) % 2)
            else:
                stage_a(j_next, 0, 0)
            stage_b(j, h, h % 2, masked)
            if h > 0:
                stage_c(j, h - 1, (h - 1) % 2)
            else:
                stage_c(j_prev, last, last % 2)

    def exact_path():
        m_ref[...] = jnp.full(m_ref.shape, NEG_INF, F32)
        acc_ref[...] = jnp.zeros(acc_ref.shape, F32)
        p_ref[last % 2] = jnp.zeros((tk, tq), BF16)
        alpha_ref[last] = jnp.ones((1, tq), F32)
        run_tiles(lambda j, masked: stage_a(j, 0, 0), exact_tile,
                  lambda j: stage_c(j, last, last % 2))

    bounded_path()
    denom = jnp.concatenate([acc_ref[h, den:den + 1, :] for h in range(N_HEADS)], axis=0)
    pl.when(jnp.logical_not(jnp.min(denom) >= L_MIN))(exact_path)

    for h in range(N_HEADS):
        o_t = acc_ref[h, 0:den, :] / acc_ref[h, den:den + 1, :]
        oT_ref[0, h * V_HEAD:(h + 1) * V_HEAD, :] = o_t.astype(BF16)


def _attention(qT, qn, kmax, kcat, vT, *, tq, tk, q_offset, n_keys):
    bsz, _, _, t_q = qT.shape
    t_k = kcat.shape[1]
    n_tiles = t_k // tk
    kern = functools.partial(_attention_kernel, tq=tq, tk=tk, q_offset=q_offset, n_keys=n_keys,
                             n_tiles=n_tiles)
    resident = pl.Buffered(1)
    return pl.pallas_call(
        kern,
        grid=(bsz, t_q // tq),
        in_specs=[
            pl.BlockSpec(memory_space=pltpu.SMEM),
            pl.BlockSpec((1, N_HEADS, QK_CAT, tq), lambda b, i: (b, 0, 0, i)),
            pl.BlockSpec((1, N_HEADS, tq), lambda b, i: (b, 0, i)),
            pl.BlockSpec((1, t_k, QK_CAT), lambda b, i: (b, 0, 0), pipeline_mode=resident),
            pl.BlockSpec((1, n_tiles, N_HEADS, VH_ROWS, tk), lambda b, i: (b, 0, 0, 0, 0),
                         pipeline_mode=resident),
        ],
        out_specs=pl.BlockSpec((1, N_HEADS * V_HEAD, tq), lambda b, i: (b, 0, i)),
        out_shape=jax.ShapeDtypeStruct((bsz, N_HEADS * V_HEAD, t_q), BF16),
        scratch_shapes=[
            pltpu.VMEM((N_HEADS, 1, tq), F32),
            pltpu.VMEM((N_HEADS, 1, tq), F32),
            pltpu.VMEM((N_HEADS, VH_ROWS, tq), F32),
            pltpu.VMEM((2, tk, tq), F32),
            pltpu.VMEM((2, 1, tq), F32),
            pltpu.VMEM((2, tk, tq), BF16),
        ],
        compiler_params=_params("parallel", "arbitrary"),
        name="attention",
    )(kmax, qT, qn, kcat, vT)


def _sample_attention_kernel(qT_ref, cc_ref, ckr_ref, cn_ref, krn_ref, wuvT_ref, oT_ref, *, n_s, past):
    lanes = N_HEADS * n_s
    q_t = qT_ref[0]
    cc = cc_ref[0].astype(BF16)
    cn = cn_ref[0].astype(BF16)
    k_cached = jnp.concatenate([cc, ckr_ref[0].astype(BF16)], axis=1)
    k_new = jnp.concatenate([cn, krn_ref[0].astype(BF16)], axis=1)
    s_c = jnp.dot(k_cached, q_t, preferred_element_type=F32)
    s_n = jnp.dot(k_new, q_t, preferred_element_type=F32)

    q_idx = lax.rem(lax.broadcasted_iota(jnp.int32, (1, lanes), 1), n_s)
    qchunk = lax.shift_right_arithmetic(past + q_idx, CHUNK_SHIFT)
    kchunk_c = lax.shift_right_arithmetic(lax.broadcasted_iota(jnp.int32, (past, 1), 0), CHUNK_SHIFT)
    kchunk_n = lax.shift_right_arithmetic(past + lax.broadcasted_iota(jnp.int32, (n_s, 1), 0), CHUNK_SHIFT)
    s_c = jnp.where(kchunk_c <= qchunk, s_c, NEG_INF)
    s_n = jnp.where(kchunk_n <= qchunk, s_n, NEG_INF)

    m = jnp.maximum(jnp.max(s_c, axis=0, keepdims=True), jnp.max(s_n, axis=0, keepdims=True))
    p_c = jnp.exp2(s_c - m)
    p_n = jnp.exp2(s_n - m)
    denom = jnp.sum(p_c, axis=0, keepdims=True) + jnp.sum(p_n, axis=0, keepdims=True)
    contract_rows = (((0,), (0,)), ((), ()))
    att_t = (lax.dot_general(cc, p_c.astype(BF16), contract_rows, preferred_element_type=F32)
             + lax.dot_general(cn, p_n.astype(BF16), contract_rows, preferred_element_type=F32)) / denom
    att_t = att_t.astype(BF16)
    for h in range(N_HEADS):
        o_t = jnp.dot(wuvT_ref[h], att_t[:, h * n_s:(h + 1) * n_s], preferred_element_type=F32)
        oT_ref[0, h * V_HEAD:(h + 1) * V_HEAD, :] = o_t.astype(BF16)


def _sample_attention(qT, cache_c, cache_kr, c_new, kr_new, wuvT):
    bsz, past, _ = cache_c.shape
    n_s = c_new.shape[1]
    kern = functools.partial(_sample_attention_kernel, n_s=n_s, past=past)
    return pl.pallas_call(
        kern,
        grid=(bsz,),
        in_specs=[
            pl.BlockSpec((1, QK_CAT, N_HEADS * n_s), lambda b: (b, 0, 0)),
            pl.BlockSpec((1, past, KV_LORA), lambda b: (b, 0, 0)),
            pl.BlockSpec((1, past, QK_ROPE), lambda b: (b, 0, 0)),
            pl.BlockSpec((1, n_s, KV_LORA), lambda b: (b, 0, 0)),
            pl.BlockSpec((1, n_s, QK_ROPE), lambda b: (b, 0, 0)),
            _const_spec((N_HEADS, V_HEAD, KV_LORA)),
        ],
        out_specs=pl.BlockSpec((1, N_HEADS * V_HEAD, n_s), lambda b: (b, 0, 0)),
        out_shape=jax.ShapeDtypeStruct((bsz, N_HEADS * V_HEAD, n_s), BF16),
        compiler_params=_params("parallel"),
        name="sample_attention",
    )(qT, cache_c, cache_kr, c_new, kr_new, wuvT)


def _layer_norm(r, g, b):
    mu = jnp.mean(r, axis=-1, keepdims=True)
    cen = r - mu
    var = jnp.mean(cen * cen, axis=-1, keepdims=True)
    return cen * lax.rsqrt(var + LN_EPS) * g + b


def _mix_kernel(x_ref, oT_ref, u_ref, pre_ref, wg_ref, bg_ref, wup_ref, wpool_ref, pscale_ref, wo_ref,
                g1_ref, b1_ref, h_ref, ext_ref, d_ref, *, seg_len, n_seg, pos0_fn, alpha):
    d_model = x_ref.shape[-1]
    x = x_ref[...]
    xb = x.astype(BF16)
    zg = jnp.dot(xb, wg_ref[...], preferred_element_type=F32) + bg_ref[...]
    gates = 1.0 / (1.0 + jnp.exp(-zg))

    pos0 = pos0_fn(pl.program_id(0))
    row = lax.broadcasted_iota(jnp.int32, (seg_len, 1), 0)
    for s in range(n_seg):
        ext_ref[0:POOL_HALO, :] = pre_ref[s]
        ext_ref[POOL_HALO:POOL_HALO + seg_len, :] = u_ref[s * seg_len:(s + 1) * seg_len, :]
        for g, w in enumerate(POOL_WINDOWS):
            lanes = slice(g * POOL_GROUP_DIM, (g + 1) * POOL_GROUP_DIM)
            cur = ext_ref[POOL_HALO:POOL_HALO + seg_len, lanes]
            acc = cur
            for k in range(1, w):
                acc = acc + ext_ref[POOL_HALO - k:POOL_HALO - k + seg_len, lanes]
            cnt = jnp.minimum(pos0 + row + 1, w).astype(F32)
            d_ref[s * seg_len:(s + 1) * seg_len, lanes] = (acc * (1.0 / cnt) - cur).astype(BF16)

    pool_out = d_model // POOL_GROUPS
    parts = []
    for g in range(POOL_GROUPS):
        dg = d_ref[:, g * POOL_GROUP_DIM:(g + 1) * POOL_GROUP_DIM]
        parts.append(jnp.dot(dg, wpool_ref[g], preferred_element_type=F32))
    p_branch = jnp.concatenate(parts, axis=-1) * pscale_ref[...]

    a_branch = lax.dot_general(oT_ref[...], wup_ref[...], (((0,), (0,)), ((), ())),
                               preferred_element_type=F32)
    m = gates[:, :d_model] * a_branch + gates[:, d_model:] * p_branch
    r = alpha * x + jnp.dot(m.astype(BF16), wo_ref[...], preferred_element_type=F32)
    h_ref[...] = _layer_norm(r, g1_ref[...], b1_ref[...])


def _mix(x2, oT, u2, prefix, wg, bg, wup, wpool, pscale, wo, g1, b1, *, seg_len, pos0_fn, alpha):
    n, d = x2.shape
    tm = TOKEN_TILE
    n_seg = tm // seg_len
    _, hv, t_o = oT.shape
    tiles_per_row = t_o // tm
    kern = functools.partial(_mix_kernel, seg_len=seg_len, n_seg=n_seg, pos0_fn=pos0_fn, alpha=alpha)
    return pl.pallas_call(
        kern,
        grid=(n // tm,),
        in_specs=[
            pl.BlockSpec((tm, d), lambda i: (i, 0)),
            pl.BlockSpec((None, hv, tm), lambda i: (i // tiles_per_row, 0, i % tiles_per_row)),
            pl.BlockSpec((tm, POOL_WIDTH), lambda i: (i, 0)),
            pl.BlockSpec((n_seg, POOL_HALO, POOL_WIDTH), lambda i: (i, 0, 0)),
            _const_spec(wg.shape), _const_spec(bg.shape), _const_spec(wup.shape),
            _const_spec(wpool.shape), _const_spec(pscale.shape), _const_spec(wo.shape),
            _const_spec(g1.shape), _const_spec(b1.shape),
        ],
        out_specs=pl.BlockSpec((tm, d), lambda i: (i, 0)),
        out_shape=jax.ShapeDtypeStruct((n, d), F32),
        scratch_shapes=[
            pltpu.VMEM((POOL_HALO + seg_len, POOL_WIDTH), F32),
            pltpu.VMEM((tm, POOL_WIDTH), BF16),
        ],
        compiler_params=_params("parallel"),
        name="mix",
    )(x2, oT, u2, prefix, wg, bg, wup, wpool, pscale, wo, g1, b1)


def _ffn_kernel(h_ref, wgate_ref, wupf_ref, wdown_ref, g2_ref, b2_ref, y_ref, *, alpha):
    h = h_ref[...]
    hb = h.astype(BF16)
    f = None
    lo = 0
    for width in FF_CHUNKS:
        gate = jnp.dot(hb, wgate_ref[:, lo:lo + width], preferred_element_type=F32)
        up = jnp.dot(hb, wupf_ref[:, lo:lo + width], preferred_element_type=F32)
        act = (gate / (1.0 + jnp.exp(-gate)) * up).astype(BF16)
        part = jnp.dot(act, wdown_ref[lo:lo + width, :], preferred_element_type=F32)
        f = part if f is None else f + part
        lo += width
    y_ref[...] = _layer_norm(alpha * h + f, g2_ref[...], b2_ref[...])


def _ffn(h2, wgate, wupf, wdown, g2, b2, *, alpha):
    n, d = h2.shape
    tm = TOKEN_TILE
    assert sum(FF_CHUNKS) == wgate.shape[1]
    return pl.pallas_call(
        functools.partial(_ffn_kernel, alpha=alpha),
        grid=(n // tm,),
        in_specs=[
            pl.BlockSpec((tm, d), lambda i: (i, 0)),
            _const_spec(wgate.shape), _const_spec(wupf.shape), _const_spec(wdown.shape),
            _const_spec(g2.shape), _const_spec(b2.shape),
        ],
        out_specs=pl.BlockSpec((tm, d), lambda i: (i, 0)),
        out_shape=jax.ShapeDtypeStruct((n, d), F32),
        compiler_params=_params("parallel"),
        name="ffn",
    )(h2, wgate, wupf, wdown, g2, b2)


def _prompt_pos0(tiles_per_seq, i):
    return (i % tiles_per_seq) * TOKEN_TILE


def _sample_pos0(past, i):
    return past


def kernel(x_prompt, x_sample, cache_kv_latent, cache_k_rope, state_pool, w_in, b_gate, q_norm_g, w_uq,
           kv_norm_g, w_uk, w_uv, w_attn_up, w_pool, pool_scale, w_o, ln1_g, ln1_b, w_gate, w_up, w_down,
           ln2_g, ln2_b):
    depth = w_in.shape[0]
    bp, n_p, d = x_prompt.shape
    bs, n_s, _ = x_sample.shape
    past = cache_kv_latent.shape[2]
    alpha = (2 * depth) ** 0.25
    half = QK_ROPE // 2
    o1 = Q_LORA
    o2 = o1 + KV_LORA
    o3 = o2 + QK_ROPE
    o4 = o3 + POOL_WIDTH

    pos_p = jnp.arange(n_p, dtype=jnp.int32)
    pos_s = past + (jnp.arange(bs * n_s, dtype=jnp.int32) % n_s)

    h_p, h_s = x_prompt, x_sample
    c_p_list, kr_p_list, pool_p_list = [], [], []
    c_s_list, kr_s_list, pool_s_list = [], [], []
    for l in range(depth):
        wl = w_in[l]
        w_kr = wl[:, o2:o3]
        w_kr_rot = jnp.concatenate([-w_kr[:, half:], w_kr[:, :half]], axis=1)
        w1 = jnp.concatenate([wl[:, :o2], wl[:, o3:o4], w_kr, w_kr_rot], axis=1).astype(BF16)
        wg = wl[:, o4:].astype(BF16)
        bg = b_gate[l][None, :]
        wq = _fold_q(w_uq[l], w_uk[l])
        qg = q_norm_g[l][None, :]
        kvg = kv_norm_g[l][None, :]
        wuvT = jnp.transpose(w_uv[l], (1, 2, 0)).astype(BF16)
        wuv = wuvT.reshape(N_HEADS * V_HEAD, KV_LORA)
        wup = w_attn_up[l].astype(BF16)
        wpool = w_pool[l].astype(BF16)
        pscale = pool_scale[l][None, :]
        wo = w_o[l].astype(BF16)
        g1, b1 = ln1_g[l][None, :], ln1_b[l][None, :]
        g2, b2 = ln2_g[l][None, :], ln2_b[l][None, :]
        wgate, wupf, wdown = w_gate[l].astype(BF16), w_up[l].astype(BF16), w_down[l].astype(BF16)

        qT, kcat, vT, c, kr, u, qn, ksq = _mixer_in(h_p, pos_p, w1, wq, wuv, qg, kvg)
        kmax = jnp.sqrt(jnp.max(ksq[:, :, 0, 0], axis=1))
        oT = _attention(qT, qn, kmax, kcat, vT, tq=Q_TILE, tk=KV_TILE, q_offset=0, n_keys=n_p)
        tiles_per_seq = n_p // TOKEN_TILE
        tails = u.reshape(bp, tiles_per_seq, TOKEN_TILE, POOL_WIDTH)[:, :, TOKEN_TILE - POOL_HALO:, :]
        prefix = jnp.concatenate([jnp.zeros_like(tails[:, :1]), tails[:, :-1]], axis=1)
        prefix = prefix.reshape(bp * tiles_per_seq, POOL_HALO, POOL_WIDTH)
        hmid = _mix(h_p.reshape(bp * n_p, d), oT, u.reshape(bp * n_p, POOL_WIDTH), prefix,
                    wg, bg, wup, wpool, pscale, wo, g1, b1, seg_len=TOKEN_TILE,
                    pos0_fn=functools.partial(_prompt_pos0, tiles_per_seq), alpha=alpha)
        y = _ffn(hmid, wgate, wupf, wdown, g2, b2, alpha=alpha)
        c_p_list.append(c)
        kr_p_list.append(kr)
        pool_p_list.append(u[:, n_p - POOL_STATE:])
        h_p = y.reshape(bp, n_p, d)

        xs = h_s.reshape(1, bs * n_s, d)
        qT, _, _, c, kr, u, _, _ = _mixer_in(xs, pos_s, w1, wq, wuv, qg, kvg)
        c = c.reshape(bs, n_s, KV_LORA)
        kr = kr.reshape(bs, n_s, QK_ROPE)
        u = u.reshape(bs, n_s, POOL_WIDTH)
        qT = jnp.transpose(qT.reshape(N_HEADS, QK_CAT, bs, n_s), (2, 1, 0, 3))
        qT = qT.reshape(bs, QK_CAT, N_HEADS * n_s)
        oT = _sample_attention(qT, cache_kv_latent[l], cache_k_rope[l], c, kr, wuvT)
        oT2 = jnp.transpose(oT, (1, 0, 2)).reshape(1, N_HEADS * V_HEAD, bs * n_s)
        prefix = jnp.pad(state_pool[l], ((0, 0), (POOL_HALO - POOL_STATE, 0), (0, 0)))
        hmid = _mix(h_s.reshape(bs * n_s, d), oT2, u.reshape(bs * n_s, POOL_WIDTH), prefix,
                    wg, bg, wup, wpool, pscale, wo, g1, b1, seg_len=n_s,
                    pos0_fn=functools.partial(_sample_pos0, past), alpha=alpha)
        y = _ffn(hmid, wgate, wupf, wdown, g2, b2, alpha=alpha)
        u_all = jnp.concatenate([state_pool[l], u], axis=1)
        c_s_list.append(c)
        kr_s_list.append(kr)
        pool_s_list.append(u_all[:, u_all.shape[1] - POOL_STATE:])
        h_s = y.reshape(bs, n_s, d)

    return (h_p, h_s, jnp.stack(c_p_list), jnp.stack(kr_p_list), jnp.stack(pool_p_list),
            jnp.stack(c_s_list), jnp.stack(kr_s_list), jnp.stack(pool_s_list))
```

```python
import functools
import math

import jax
import jax.numpy as jnp
from jax import lax
from jax.experimental import pallas as pl
from jax.experimental.pallas import tpu as pltpu

F32 = jnp.float32
BF16 = jnp.bfloat16

N_HEADS = 8
QK_NOPE = 64
QK_ROPE = 32
QK_HEAD = QK_NOPE + QK_ROPE
V_HEAD = 64
Q_LORA = 256
KV_LORA = 128
QK_CAT = KV_LORA + QK_ROPE
Q_ROWS = KV_LORA + 2 * QK_ROPE
VH_ROWS = V_HEAD + 16
CHUNK = 64
CHUNK_SHIFT = 6
ROPE_THETA = 10000.0
ATTN_SCALE = QK_HEAD ** -0.5
POOL_WIDTH = 512
POOL_GROUPS = 4
POOL_GROUP_DIM = POOL_WIDTH // POOL_GROUPS
POOL_WINDOWS = (2, 4, 8, 16)
POOL_STATE = max(POOL_WINDOWS) - 1
POOL_HALO = POOL_STATE + 1
RMS_EPS = 1e-6
LN_EPS = 1e-5
NEG_INF = -1e30
L_MIN = 2.0 ** -60
Q_SCALE = ATTN_SCALE * math.log2(math.e)

VMEM_LIMIT_BYTES = 56 * 1024 * 1024

TOKEN_TILE = 512
KV_TILE = 512
Q_TILE = 512
FF_CHUNKS = (768, 768, 768, 512)


def _const_spec(shape):
    nd = len(shape)
    return pl.BlockSpec(shape, lambda *_: (0,) * nd, pipeline_mode=pl.Buffered(1))


def _params(*sem):
    return pltpu.CompilerParams(dimension_semantics=sem, vmem_limit_bytes=VMEM_LIMIT_BYTES)


def _fold_q_kernel(wuk_ref, wuqn_ref, wuqr_ref, out_ref):
    half = QK_ROPE // 2
    for h in range(N_HEADS):
        lat = jnp.dot(wuk_ref[h], wuqn_ref[h], preferred_element_type=F32,
                      precision=lax.Precision.HIGHEST)
        rope = wuqr_ref[h]
        out_ref[h, 0:KV_LORA, :] = lat.astype(BF16)
        out_ref[h, KV_LORA:KV_LORA + QK_ROPE, :] = rope.astype(BF16)
        out_ref[h, KV_LORA + QK_ROPE:KV_LORA + QK_ROPE + half, :] = (-rope[half:]).astype(BF16)
        out_ref[h, KV_LORA + QK_ROPE + half:Q_ROWS, :] = rope[:half].astype(BF16)


def _fold_q(w_uq, w_uk):
    wq = w_uq.reshape(Q_LORA, N_HEADS, QK_HEAD)
    wuqn_t = jnp.transpose(wq[:, :, :QK_NOPE], (1, 2, 0))
    wuqr_t = jnp.transpose(wq[:, :, QK_NOPE:], (1, 2, 0))
    wuk_t = jnp.transpose(w_uk, (1, 0, 2))
    out = pl.pallas_call(
        _fold_q_kernel,
        out_shape=jax.ShapeDtypeStruct((N_HEADS, Q_ROWS, Q_LORA), BF16),
        name="fold_q",
    )(wuk_t, wuqn_t, wuqr_t)
    return out.reshape(N_HEADS * Q_ROWS, Q_LORA)


def _mixer_in_kernel(x_ref, w1_ref, wq_ref, wuv_ref, qg_ref, kvg_ref, cosr_ref, sinr_ref, cost_ref, sint_ref,
                     qT_ref, kcat_ref, vT_ref, ckv_ref, kr_ref, u_ref, qn_ref, ksq_ref, *, stream_len):
    xb = x_ref[...].astype(BF16)
    z = jnp.dot(xb, w1_ref[...], preferred_element_type=F32)
    o1 = Q_LORA
    o2 = o1 + KV_LORA
    o3 = o2 + POOL_WIDTH
    o4 = o3 + QK_ROPE
    zq, zkv, u = z[:, :o1], z[:, o1:o2], z[:, o2:o3]
    zkr, zkr_rot = z[:, o3:o4], z[:, o4:o4 + QK_ROPE]
    u_ref[0] = u

    c = zkv * lax.rsqrt(jnp.mean(zkv * zkv, axis=-1, keepdims=True) + RMS_EPS) * kvg_ref[...]
    ckv_ref[0] = c
    kr = zkr * cosr_ref[...] + zkr_rot * sinr_ref[...]
    kr_ref[0] = kr
    cb = c.astype(BF16)
    krb = kr.astype(BF16)
    kcat_ref[0, :, 0:KV_LORA] = cb
    kcat_ref[0, :, KV_LORA:QK_CAT] = krb
    cf = cb.astype(F32)
    krf = krb.astype(F32)
    k_sq = jnp.sum(cf * cf, axis=-1, keepdims=True) + jnp.sum(krf * krf, axis=-1, keepdims=True)
    ksq_ref[0, 0] = jnp.broadcast_to(jnp.max(k_sq, axis=0, keepdims=True), ksq_ref.shape[2:])
    v_t = jnp.dot(wuv_ref[...], c.T.astype(BF16), preferred_element_type=F32)
    ones_row = lax.broadcasted_iota(jnp.int32, (VH_ROWS - V_HEAD, c.shape[0]), 0) == 0
    ones_rows = jnp.where(ones_row, 1.0, 0.0).astype(BF16)
    for h in range(N_HEADS):
        vT_ref[0, 0, h, 0:V_HEAD, :] = v_t[h * V_HEAD:(h + 1) * V_HEAD].astype(BF16)
        vT_ref[0, 0, h, V_HEAD:VH_ROWS, :] = ones_rows

    zqn = zq * lax.rsqrt(jnp.mean(zq * zq, axis=-1, keepdims=True) + RMS_EPS) * (qg_ref[...] * Q_SCALE)
    zqn_t = zqn.T.astype(BF16)
    q_all = jnp.dot(wq_ref[...], zqn_t, preferred_element_type=F32)
    cos_t = cost_ref[...]
    sin_t = sint_ref[...]
    for h in range(N_HEADS):
        b = h * Q_ROWS
        lat = q_all[b:b + KV_LORA].astype(BF16)
        rope = (q_all[b + KV_LORA:b + KV_LORA + QK_ROPE] * cos_t
                + q_all[b + KV_LORA + QK_ROPE:b + Q_ROWS] * sin_t).astype(BF16)
        if stream_len is None:
            qT_ref[0, h, 0:KV_LORA, :] = lat
            qT_ref[0, h, KV_LORA:QK_CAT, :] = rope
        else:
            for s in range(lat.shape[1] // stream_len):
                src = slice(s * stream_len, (s + 1) * stream_len)
                dst = slice(h * stream_len, (h + 1) * stream_len)
                qT_ref[s, 0:KV_LORA, dst] = lat[:, src]
                qT_ref[s, KV_LORA:QK_CAT, dst] = rope[:, src]
        latf = lat.astype(F32)
        ropef = rope.astype(F32)
        q_sq = jnp.sum(latf * latf, axis=0, keepdims=True) + jnp.sum(ropef * ropef, axis=0, keepdims=True)
        qn_ref[0, h:h + 1, :] = jnp.sqrt(q_sq)


def _mixer_in(x, pos, w1, wq, wuv, qg, kvg, stream_len=None):
    bsz, t, d = x.shape
    tm = TOKEN_TILE
    nt = t // tm
    if stream_len is None:
        q_spec = pl.BlockSpec((1, N_HEADS, QK_CAT, tm), lambda b, i: (b, 0, 0, i))
        q_shape = jax.ShapeDtypeStruct((bsz, N_HEADS, QK_CAT, t), BF16)
    else:
        per_tile = tm // stream_len
        q_spec = pl.BlockSpec((per_tile, QK_CAT, N_HEADS * stream_len), lambda b, i: (b * nt + i, 0, 0))
        q_shape = jax.ShapeDtypeStruct((bsz * t // stream_len, QK_CAT, N_HEADS * stream_len), BF16)
    half = QK_ROPE // 2
    inv = 1.0 / (ROPE_THETA ** (jnp.arange(half, dtype=F32) / half))
    ang = pos.astype(F32)[:, None] * inv[None, :]
    cos_r = jnp.tile(jnp.cos(ang), (1, 2))
    sin_r = jnp.tile(jnp.sin(ang), (1, 2))
    cos_t, sin_t = cos_r.T, sin_r.T
    n1 = w1.shape[1]
    outs = pl.pallas_call(
        functools.partial(_mixer_in_kernel, stream_len=stream_len),
        grid=(bsz, nt),
        in_specs=[
            pl.BlockSpec((None, tm, d), lambda b, i: (b, i, 0)),
            _const_spec((d, n1)),
            _const_spec((N_HEADS * Q_ROWS, Q_LORA)),
            _const_spec((N_HEADS * V_HEAD, KV_LORA)),
            _const_spec((1, Q_LORA)),
            _const_spec((1, KV_LORA)),
            pl.BlockSpec((tm, QK_ROPE), lambda b, i: (i, 0)),
            pl.BlockSpec((tm, QK_ROPE), lambda b, i: (i, 0)),
            pl.BlockSpec((QK_ROPE, tm), lambda b, i: (0, i)),
            pl.BlockSpec((QK_ROPE, tm), lambda b, i: (0, i)),
        ],
        out_specs=[
            q_spec,
            pl.BlockSpec((1, tm, QK_CAT), lambda b, i: (b, i, 0)),
            pl.BlockSpec((1, 1, N_HEADS, VH_ROWS, tm), lambda b, i: (b, i, 0, 0, 0)),
            pl.BlockSpec((1, tm, KV_LORA), lambda b, i: (b, i, 0)),
            pl.BlockSpec((1, tm, QK_ROPE), lambda b, i: (b, i, 0)),
            pl.BlockSpec((1, tm, POOL_WIDTH), lambda b, i: (b, i, 0)),
            pl.BlockSpec((1, N_HEADS, tm), lambda b, i: (b, 0, i)),
            pl.BlockSpec((1, 1, 8, 128), lambda b, i: (b, i, 0, 0)),
        ],
        out_shape=[
            q_shape,
            jax.ShapeDtypeStruct((bsz, t, QK_CAT), BF16),
            jax.ShapeDtypeStruct((bsz, nt, N_HEADS, VH_ROWS, tm), BF16),
            jax.ShapeDtypeStruct((bsz, t, KV_LORA), F32),
            jax.ShapeDtypeStruct((bsz, t, QK_ROPE), F32),
            jax.ShapeDtypeStruct((bsz, t, POOL_WIDTH), F32),
            jax.ShapeDtypeStruct((bsz, N_HEADS, t), F32),
            jax.ShapeDtypeStruct((bsz, nt, 8, 128), F32),
        ],
        compiler_params=_params("parallel", "parallel"),
        name="mixer_in",
    )(x, w1, wq, wuv, qg, kvg, cos_r, sin_r, cos_t, sin_t)
    return outs


def _attention_kernel(kmax_ref, qT_ref, qn_ref, kcat_ref, vT_ref, oT_ref, m_ref, alpha_ref, acc_ref,
                      s_ref, mt_ref, p_ref, *, tq, tk, q_offset, n_keys, n_tiles):
    den = V_HEAD

    def values(j, h):
        return vT_ref[0, j, h]

    i = pl.program_id(1)
    q0 = q_offset + i * tq
    first_end = (q0 // CHUNK + 1) * CHUNK
    last_end = ((q0 + tq - 1) // CHUNK + 1) * CHUNK
    n_all = (jnp.minimum(last_end, n_keys) + tk - 1) // tk
    n_full = jnp.minimum(first_end // tk, n_all)
    last = N_HEADS - 1

    def scores(j, h):
        k = kcat_ref[0, pl.ds(pl.multiple_of(j * tk, tk), tk), :]
        return jnp.dot(k, qT_ref[0, h], preferred_element_type=F32)

    def visible(j):
        kchunk = lax.shift_right_arithmetic(j * tk + lax.broadcasted_iota(jnp.int32, (tk, 1), 0), CHUNK_SHIFT)
        qchunk = lax.shift_right_arithmetic(q0 + lax.broadcasted_iota(jnp.int32, (1, tq), 1), CHUNK_SHIFT)
        return kchunk <= qchunk

    def run_tiles(first_stage, tile_body, final_stage, pair_tiles=False):
        first_stage(0, False)
        n_single = 0
        if pair_tiles:
            def pair_body(t, c):
                tile_body(2 * t, False)
                tile_body(2 * t + 1, False)
                return c

            lax.fori_loop(0, n_full // 2, pair_body, 0)
            n_single = (n_full // 2) * 2
        lax.fori_loop(n_single, n_full, lambda j, c: tile_body(j, False) or c, 0)
        first_stage(jnp.minimum(n_full, n_tiles - 1), True)
        lax.fori_loop(n_full, n_all, lambda j, c: tile_body(j, True) or c, 0)
        if final_stage is not None:
            final_stage(n_all - 1)

    def stage_p(j, h, slot, masked):
        x = scores(j, h) - qn_ref[0, h:h + 1, :] * kmax_ref[pl.program_id(0)]
        if masked:
            x = jnp.where(visible(j), x, NEG_INF)
        p_ref[slot] = jnp.exp2(x).astype(BF16)

    def stage_v(j, h, slot):
        acc_ref[h] += jnp.dot(values(j, h), p_ref[slot], preferred_element_type=F32)

    def bounded_tile(j, masked):
        j_next = jnp.minimum(j + 1, n_tiles - 1)
        for h in range(N_HEADS):
            if h < last:
                stage_p(j, h + 1, (h + 1) % 2, masked)
            else:
                stage_p(j_next, 0, 0, masked)
            stage_v(j, h, h % 2)

    def bounded_path():
        acc_ref[...] = jnp.zeros(acc_ref.shape, F32)
        run_tiles(lambda j, masked: stage_p(j, 0, 0, masked), bounded_tile, None, pair_tiles=True)

    def stage_a(j, h, slot):
        s = scores(j, h)
        s_ref[slot] = s
        mt_ref[slot] = jnp.max(s, axis=0, keepdims=True)

    def stage_b(j, h, slot, masked):
        s = s_ref[slot]
        mt = mt_ref[slot]
        if masked:
            s = jnp.where(visible(j), s, NEG_INF)
            mt = jnp.max(s, axis=0, keepdims=True)
        m_old = m_ref[h]
        m_new = jnp.maximum(m_old, mt)
        m_ref[h] = m_new
        alpha_ref[h] = jnp.exp2(m_old - m_new)
        p_ref[slot] = jnp.exp2(s - m_new).astype(BF16)

    def stage_c(j, h, slot):
        pv = jnp.dot(values(j, h), p_ref[slot], preferred_element_type=F32)
        acc_ref[h] = alpha_ref[h] * acc_ref[h] + pv

    def exact_tile(j, masked):
        j_next = jnp.minimum(j + 1, n_tiles - 1)
        j_prev = jnp.maximum(j - 1, 0)
        for h in range(N_HEADS):
            if h < last:
                stage_a(j, h + 1, (h + 1) % 2)
            else:
                stage_a(j_next, 0, 0)
            stage_b(j, h, h % 2, masked)
            if h > 0:
                stage_c(j, h - 1, (h - 1) % 2)
            else:
                stage_c(j_prev, last, last % 2)

    def exact_path():
        m_ref[...] = jnp.full(m_ref.shape, NEG_INF, F32)
        acc_ref[...] = jnp.zeros(acc_ref.shape, F32)
        p_ref[last % 2] = jnp.zeros((tk, tq), BF16)
        alpha_ref[last] = jnp.ones((1, tq), F32)
        run_tiles(lambda j, masked: stage_a(j, 0, 0), exact_tile,
                  lambda j: stage_c(j, last, last % 2))

    bounded_path()
    denom = jnp.concatenate([acc_ref[h, den:den + 1, :] for h in range(N_HEADS)], axis=0)
    pl.when(jnp.logical_not(jnp.min(denom) >= L_MIN))(exact_path)

    for h in range(N_HEADS):
        o_t = acc_ref[h, 0:den, :] / acc_ref[h, den:den + 1, :]
        oT_ref[0, h * V_HEAD:(h + 1) * V_HEAD, :] = o_t.astype(BF16)


def _attention(qT, qn, kmax, kcat, vT, *, tq, tk, q_offset, n_keys):
    bsz, _, _, t_q = qT.shape
    t_k = kcat.shape[1]
    n_tiles = t_k // tk
    kern = functools.partial(_attention_kernel, tq=tq, tk=tk, q_offset=q_offset, n_keys=n_keys,
                             n_tiles=n_tiles)
    resident = pl.Buffered(1)
    return pl.pallas_call(
        kern,
        grid=(bsz, t_q // tq),
        in_specs=[
            pl.BlockSpec(memory_space=pltpu.SMEM),
            pl.BlockSpec((1, N_HEADS, QK_CAT, tq), lambda b, i: (b, 0, 0, i)),
            pl.BlockSpec((1, N_HEADS, tq), lambda b, i: (b, 0, i)),
            pl.BlockSpec((1, t_k, QK_CAT), lambda b, i: (b, 0, 0), pipeline_mode=resident),
            pl.BlockSpec((1, n_tiles, N_HEADS, VH_ROWS, tk), lambda b, i: (b, 0, 0, 0, 0),
                         pipeline_mode=resident),
        ],
        out_specs=pl.BlockSpec((1, N_HEADS * V_HEAD, tq), lambda b, i: (b, 0, i)),
        out_shape=jax.ShapeDtypeStruct((bsz, N_HEADS * V_HEAD, t_q), BF16),
        scratch_shapes=[
            pltpu.VMEM((N_HEADS, 1, tq), F32),
            pltpu.VMEM((N_HEADS, 1, tq), F32),
            pltpu.VMEM((N_HEADS, VH_ROWS, tq), F32),
            pltpu.VMEM((2, tk, tq), F32),
            pltpu.VMEM((2, 1, tq), F32),
            pltpu.VMEM((2, tk, tq), BF16),
        ],
        compiler_params=_params("parallel", "arbitrary"),
        name="attention",
    )(kmax, qT, qn, kcat, vT)


def _sample_attention_kernel(qT_ref, cc_ref, ckr_ref, cn_ref, krn_ref, wuvT_ref, oT_ref, *, n_s, past):
    lanes = N_HEADS * n_s
    q_t = qT_ref[0]
    cc = cc_ref[0].astype(BF16)
    cn = cn_ref[0].astype(BF16)
    k_cached = jnp.concatenate([cc, ckr_ref[0].astype(BF16)], axis=1)
    k_new = jnp.concatenate([cn, krn_ref[0].astype(BF16)], axis=1)
    s_c = jnp.dot(k_cached, q_t, preferred_element_type=F32)
    s_n = jnp.dot(k_new, q_t, preferred_element_type=F32)

    q_idx = lax.rem(lax.broadcasted_iota(jnp.int32, (1, lanes), 1), n_s)
    qchunk = lax.shift_right_arithmetic(past + q_idx, CHUNK_SHIFT)
    kchunk_c = lax.shift_right_arithmetic(lax.broadcasted_iota(jnp.int32, (past, 1), 0), CHUNK_SHIFT)
    kchunk_n = lax.shift_right_arithmetic(past + lax.broadcasted_iota(jnp.int32, (n_s, 1), 0), CHUNK_SHIFT)
    s_c = jnp.where(kchunk_c <= qchunk, s_c, NEG_INF)
    s_n = jnp.where(kchunk_n <= qchunk, s_n, NEG_INF)

    m = jnp.maximum(jnp.max(s_c, axis=0, keepdims=True), jnp.max(s_n, axis=0, keepdims=True))
    p_c = jnp.exp2(s_c - m)
    p_n = jnp.exp2(s_n - m)
    denom = jnp.sum(p_c, axis=0, keepdims=True) + jnp.sum(p_n, axis=0, keepdims=True)
    contract_rows = (((0,), (0,)), ((), ()))
    att_t = (lax.dot_general(cc, p_c.astype(BF16), contract_rows, preferred_element_type=F32)
             + lax.dot_general(cn, p_n.astype(BF16), contract_rows, preferred_element_type=F32)) / denom
    att_t = att_t.astype(BF16)
    for h in range(N_HEADS):
        o_t = jnp.dot(wuvT_ref[h], att_t[:, h * n_s:(h + 1) * n_s], preferred_element_type=F32)
        oT_ref[0, h * V_HEAD:(h + 1) * V_HEAD, :] = o_t.astype(BF16)


def _sample_attention(qT, cache_c, cache_kr, c_new, kr_new, wuvT):
    bsz, past, _ = cache_c.shape
    n_s = c_new.shape[1]
    kern = functools.partial(_sample_attention_kernel, n_s=n_s, past=past)
    return pl.pallas_call(
        kern,
        grid=(bsz,),
        in_specs=[
            pl.BlockSpec((1, QK_CAT, N_HEADS * n_s), lambda b: (b, 0, 0)),
            pl.BlockSpec((1, past, KV_LORA), lambda b: (b, 0, 0)),
            pl.BlockSpec((1, past, QK_ROPE), lambda b: (b, 0, 0)),
            pl.BlockSpec((1, n_s, KV_LORA), lambda b: (b, 0, 0)),
            pl.BlockSpec((1, n_s, QK_ROPE), lambda b: (b, 0, 0)),
            _const_spec((N_HEADS, V_HEAD, KV_LORA)),
        ],
        out_specs=pl.BlockSpec((1, N_HEADS * V_HEAD, n_s), lambda b: (b, 0, 0)),
        out_shape=jax.ShapeDtypeStruct((bsz, N_HEADS * V_HEAD, n_s), BF16),
        compiler_params=_params("parallel"),
        name="sample_attention",
    )(qT, cache_c, cache_kr, c_new, kr_new, wuvT)


def _layer_norm(r, g, b):
    mu = jnp.mean(r, axis=-1, keepdims=True)
    cen = r - mu
    var = jnp.mean(cen * cen, axis=-1, keepdims=True)
    return cen * lax.rsqrt(var + LN_EPS) * g + b


def _mix_kernel(x_ref, oT_ref, u_ref, pre_ref, wg_ref, bg_ref, wup_ref, wpool_ref, pscale_ref, wo_ref,
                g1_ref, b1_ref, h_ref, ext_ref, d_ref, *, seg_len, n_seg, pos0_fn, alpha):
    d_model = x_ref.shape[-1]
    x = x_ref[...]
    xb = x.astype(BF16)
    zg = jnp.dot(xb, wg_ref[...], preferred_element_type=F32) + bg_ref[...]
    gates = 1.0 / (1.0 + jnp.exp(-zg))

    pos0 = pos0_fn(pl.program_id(0))
    row = lax.broadcasted_iota(jnp.int32, (seg_len, 1), 0)
    for s in range(n_seg):
        ext_ref[0:POOL_HALO, :] = pre_ref[s]
        ext_ref[POOL_HALO:POOL_HALO + seg_len, :] = u_ref[s * seg_len:(s + 1) * seg_len, :]
        for g, w in enumerate(POOL_WINDOWS):
            lanes = slice(g * POOL_GROUP_DIM, (g + 1) * POOL_GROUP_DIM)
            cur = ext_ref[POOL_HALO:POOL_HALO + seg_len, lanes]
            acc = cur
            for k in range(1, w):
                acc = acc + ext_ref[POOL_HALO - k:POOL_HALO - k + seg_len, lanes]
            cnt = jnp.minimum(pos0 + row + 1, w).astype(F32)
            d_ref[s * seg_len:(s + 1) * seg_len, lanes] = (acc * (1.0 / cnt) - cur).astype(BF16)

    pool_out = d_model // POOL_GROUPS
    parts = []
    for g in range(POOL_GROUPS):
        dg = d_ref[:, g * POOL_GROUP_DIM:(g + 1) * POOL_GROUP_DIM]
        parts.append(jnp.dot(dg, wpool_ref[g], preferred_element_type=F32))
    p_branch = jnp.concatenate(parts, axis=-1) * pscale_ref[...]

    wup = wup_ref[...]
    a_branch = jnp.concatenate(
        [lax.dot_general(oT_ref[s], wup, (((0,), (0,)), ((), ())), preferred_element_type=F32)
         for s in range(n_seg)], axis=0)
    m = gates[:, :d_model] * a_branch + gates[:, d_model:] * p_branch
    r = alpha * x + jnp.dot(m.astype(BF16), wo_ref[...], preferred_element_type=F32)
    h_ref[...] = _layer_norm(r, g1_ref[...], b1_ref[...])


def _mix(x2, oT, u2, prefix, wg, bg, wup, wpool, pscale, wo, g1, b1, *, seg_len, pos0_fn, alpha):
    n, d = x2.shape
    tm = TOKEN_TILE
    n_seg = tm // seg_len
    _, hv, t_o = oT.shape
    if n_seg == 1:
        tiles_per_row = t_o // tm
        o_spec = pl.BlockSpec((1, hv, tm), lambda i: (i // tiles_per_row, 0, i % tiles_per_row))
    else:
        assert t_o == seg_len
        o_spec = pl.BlockSpec((n_seg, hv, seg_len), lambda i: (i, 0, 0))
    kern = functools.partial(_mix_kernel, seg_len=seg_len, n_seg=n_seg, pos0_fn=pos0_fn, alpha=alpha)
    return pl.pallas_call(
        kern,
        grid=(n // tm,),
        in_specs=[
            pl.BlockSpec((tm, d), lambda i: (i, 0)),
            o_spec,
            pl.BlockSpec((tm, POOL_WIDTH), lambda i: (i, 0)),
            pl.BlockSpec((n_seg, POOL_HALO, POOL_WIDTH), lambda i: (i, 0, 0)),
            _const_spec(wg.shape), _const_spec(bg.shape), _const_spec(wup.shape),
            _const_spec(wpool.shape), _const_spec(pscale.shape), _const_spec(wo.shape),
            _const_spec(g1.shape), _const_spec(b1.shape),
        ],
        out_specs=pl.BlockSpec((tm, d), lambda i: (i, 0)),
        out_shape=jax.ShapeDtypeStruct((n, d), F32),
        scratch_shapes=[
            pltpu.VMEM((POOL_HALO + seg_len, POOL_WIDTH), F32),
            pltpu.VMEM((tm, POOL_WIDTH), BF16),
        ],
        compiler_params=_params("parallel"),
        name="mix",
    )(x2, oT, u2, prefix, wg, bg, wup, wpool, pscale, wo, g1, b1)


def _ffn_kernel(h_ref, wgate_ref, wupf_ref, wdown_ref, g2_ref, b2_ref, y_ref, *, alpha):
    h = h_ref[...]
    hb = h.astype(BF16)
    f = None
    lo = 0
    for width in FF_CHUNKS:
        gate = jnp.dot(hb, wgate_ref[:, lo:lo + width], preferred_element_type=F32)
        up = jnp.dot(hb, wupf_ref[:, lo:lo + width], preferred_element_type=F32)
        act = (gate / (1.0 + jnp.exp(-gate)) * up).astype(BF16)
        part = jnp.dot(act, wdown_ref[lo:lo + width, :], preferred_element_type=F32)
        f = part if f is None else f + part
        lo += width
    y_ref[...] = _layer_norm(alpha * h + f, g2_ref[...], b2_ref[...])


def _ffn(h2, wgate, wupf, wdown, g2, b2, *, alpha):
    n, d = h2.shape
    tm = TOKEN_TILE
    assert sum(FF_CHUNKS) == wgate.shape[1]
    return pl.pallas_call(
        functools.partial(_ffn_kernel, alpha=alpha),
        grid=(n // tm,),
        in_specs=[
            pl.BlockSpec((tm, d), lambda i: (i, 0)),
            _const_spec(wgate.shape), _const_spec(wupf.shape), _const_spec(wdown.shape),
            _const_spec(g2.shape), _const_spec(b2.shape),
        ],
        out_specs=pl.BlockSpec((tm, d), lambda i: (i, 0)),
        out_shape=jax.ShapeDtypeStruct((n, d), F32),
        compiler_params=_params("parallel"),
        name="ffn",
    )(h2, wgate, wupf, wdown, g2, b2)


def _prompt_pos0(tiles_per_seq, i):
    return (i % tiles_per_seq) * TOKEN_TILE


def _sample_pos0(past, i):
    return past


def kernel(x_prompt, x_sample, cache_kv_latent, cache_k_rope, state_pool, w_in, b_gate, q_norm_g, w_uq,
           kv_norm_g, w_uk, w_uv, w_attn_up, w_pool, pool_scale, w_o, ln1_g, ln1_b, w_gate, w_up, w_down,
           ln2_g, ln2_b):
    depth = w_in.shape[0]
    bp, n_p, d = x_prompt.shape
    bs, n_s, _ = x_sample.shape
    past = cache_kv_latent.shape[2]
    alpha = (2 * depth) ** 0.25
    half = QK_ROPE // 2
    o1 = Q_LORA
    o2 = o1 + KV_LORA
    o3 = o2 + QK_ROPE
    o4 = o3 + POOL_WIDTH

    pos_p = jnp.arange(n_p, dtype=jnp.int32)
    pos_s = past + (jnp.arange(bs * n_s, dtype=jnp.int32) % n_s)

    h_p, h_s = x_prompt, x_sample
    c_p_list, kr_p_list, pool_p_list = [], [], []
    c_s_list, kr_s_list, pool_s_list = [], [], []
    for l in range(depth):
        wl = w_in[l]
        w_kr = wl[:, o2:o3]
        w_kr_rot = jnp.concatenate([-w_kr[:, half:], w_kr[:, :half]], axis=1)
        w1 = jnp.concatenate([wl[:, :o2], wl[:, o3:o4], w_kr, w_kr_rot], axis=1).astype(BF16)
        wg = wl[:, o4:].astype(BF16)
        bg = b_gate[l][None, :]
        wq = _fold_q(w_uq[l], w_uk[l])
        qg = q_norm_g[l][None, :]
        kvg = kv_norm_g[l][None, :]
        wuvT = jnp.transpose(w_uv[l], (1, 2, 0)).astype(BF16)
        wuv = wuvT.reshape(N_HEADS * V_HEAD, KV_LORA)
        wup = w_attn_up[l].astype(BF16)
        wpool = w_pool[l].astype(BF16)
        pscale = pool_scale[l][None, :]
        wo = w_o[l].astype(BF16)
        g1, b1 = ln1_g[l][None, :], ln1_b[l][None, :]
        g2, b2 = ln2_g[l][None, :], ln2_b[l][None, :]
        wgate, wupf, wdown = w_gate[l].astype(BF16), w_up[l].astype(BF16), w_down[l].astype(BF16)

        qT, kcat, vT, c, kr, u, qn, ksq = _mixer_in(h_p, pos_p, w1, wq, wuv, qg, kvg)
        kmax = jnp.sqrt(jnp.max(ksq[:, :, 0, 0], axis=1))
        oT = _attention(qT, qn, kmax, kcat, vT, tq=Q_TILE, tk=KV_TILE, q_offset=0, n_keys=n_p)
        tiles_per_seq = n_p // TOKEN_TILE
        tails = u.reshape(bp, tiles_per_seq, TOKEN_TILE, POOL_WIDTH)[:, :, TOKEN_TILE - POOL_HALO:, :]
        prefix = jnp.concatenate([jnp.zeros_like(tails[:, :1]), tails[:, :-1]], axis=1)
        prefix = prefix.reshape(bp * tiles_per_seq, POOL_HALO, POOL_WIDTH)
        hmid = _mix(h_p.reshape(bp * n_p, d), oT, u.reshape(bp * n_p, POOL_WIDTH), prefix,
                    wg, bg, wup, wpool, pscale, wo, g1, b1, seg_len=TOKEN_TILE,
                    pos0_fn=functools.partial(_prompt_pos0, tiles_per_seq), alpha=alpha)
        y = _ffn(hmid, wgate, wupf, wdown, g2, b2, alpha=alpha)
        c_p_list.append(c)
        kr_p_list.append(kr)
        pool_p_list.append(u[:, n_p - POOL_STATE:])
        h_p = y.reshape(bp, n_p, d)

        xs = h_s.reshape(1, bs * n_s, d)
        qT, _, _, c, kr, u, _, _ = _mixer_in(xs, pos_s, w1, wq, wuv, qg, kvg, stream_len=n_s)
        c = c.reshape(bs, n_s, KV_LORA)
        kr = kr.reshape(bs, n_s, QK_ROPE)
        u = u.reshape(bs, n_s, POOL_WIDTH)
        oT = _sample_attention(qT, cache_kv_latent[l], cache_k_rope[l], c, kr, wuvT)
        prefix = jnp.pad(state_pool[l], ((0, 0), (POOL_HALO - POOL_STATE, 0), (0, 0)))
        hmid = _mix(h_s.reshape(bs * n_s, d), oT, u.reshape(bs * n_s, POOL_WIDTH), prefix,
                    wg, bg, wup, wpool, pscale, wo, g1, b1, seg_len=n_s,
                    pos0_fn=functools.partial(_sample_pos0, past), alpha=alpha)
        y = _ffn(hmid, wgate, wupf, wdown, g2, b2, alpha=alpha)
        u_all = jnp.concatenate([state_pool[l], u], axis=1)
        c_s_list.append(c)
        kr_s_list.append(kr)
        pool_s_list.append(u_all[:, u_all.shape[1] - POOL_STATE:])
        h_s = y.reshape(bs, n_s, d)

    def stack(parts):
        return parts[0][None] if len(parts) == 1 else jnp.stack(parts)

    return (h_p, h_s, stack(c_p_list), stack(kr_p_list), stack(pool_p_list),
            stack(c_s_list), stack(kr_s_list), stack(pool_s_list))
```

```python
import functools
import math

import jax
import jax.numpy as jnp
from jax import lax
from jax.experimental import pallas as pl
from jax.experimental.pallas import tpu as pltpu

F32 = jnp.float32
BF16 = jnp.bfloat16

LANES = 128
SUBLANES = 8
BF16_SUBLANES = 16

N_HEADS = 8
QK_NOPE = 64
QK_ROPE = 32
QK_HEAD = QK_NOPE + QK_ROPE
V_HEAD = 64
Q_LORA = 256
KV_LORA = 128
QK_CAT = KV_LORA + QK_ROPE
Q_ROWS = KV_LORA + 2 * QK_ROPE
VH_ROWS = V_HEAD + BF16_SUBLANES
CHUNK = 64
CHUNK_SHIFT = CHUNK.bit_length() - 1
ROPE_THETA = 10000.0
ATTN_SCALE = QK_HEAD ** -0.5
POOL_WIDTH = 512
POOL_GROUPS = 4
POOL_GROUP_DIM = POOL_WIDTH // POOL_GROUPS
POOL_WINDOWS = (2, 4, 8, 16)
POOL_STATE = max(POOL_WINDOWS) - 1
POOL_HALO = POOL_STATE + 1
RMS_EPS = 1e-6
LN_EPS = 1e-5
NEG_INF = -1e30
L_MIN = 2.0 ** -60
Q_SCALE = ATTN_SCALE * math.log2(math.e)

VMEM_LIMIT_BYTES = 56 * 1024 * 1024

TOKEN_TILE = 512
KV_TILE = 512
Q_TILE = 512
KV_TILES_PER_TRIP = 4
FF_CHUNKS = (768, 768, 768, 512)


def _const_spec(shape):
    nd = len(shape)
    return pl.BlockSpec(shape, lambda *_: (0,) * nd, pipeline_mode=pl.Buffered(1))


def _params(*sem):
    return pltpu.CompilerParams(dimension_semantics=sem, vmem_limit_bytes=VMEM_LIMIT_BYTES)


def _fold_q_kernel(wuk_ref, wuqn_ref, wuqr_ref, out_ref):
    half = QK_ROPE // 2
    for h in range(N_HEADS):
        lat = jnp.dot(wuk_ref[h], wuqn_ref[h], preferred_element_type=F32,
                      precision=lax.Precision.HIGHEST)
        rope = wuqr_ref[h]
        out_ref[h, 0:KV_LORA, :] = lat.astype(BF16)
        out_ref[h, KV_LORA:KV_LORA + QK_ROPE, :] = rope.astype(BF16)
        out_ref[h, KV_LORA + QK_ROPE:KV_LORA + QK_ROPE + half, :] = (-rope[half:]).astype(BF16)
        out_ref[h, KV_LORA + QK_ROPE + half:Q_ROWS, :] = rope[:half].astype(BF16)


def _fold_q(w_uq, w_uk):
    wq = w_uq.reshape(Q_LORA, N_HEADS, QK_HEAD)
    wuqn_t = jnp.transpose(wq[:, :, :QK_NOPE], (1, 2, 0))
    wuqr_t = jnp.transpose(wq[:, :, QK_NOPE:], (1, 2, 0))
    wuk_t = jnp.transpose(w_uk, (1, 0, 2))
    out = pl.pallas_call(
        _fold_q_kernel,
        out_shape=jax.ShapeDtypeStruct((N_HEADS, Q_ROWS, Q_LORA), BF16),
        name="fold_q",
    )(wuk_t, wuqn_t, wuqr_t)
    return out.reshape(N_HEADS * Q_ROWS, Q_LORA)


def _mixer_in_kernel(x_ref, w1_ref, wq_ref, wuv_ref, qg_ref, kvg_ref, cosr_ref, sinr_ref, cost_ref, sint_ref,
                     qT_ref, kcat_ref, vT_ref, ckv_ref, kr_ref, u_ref, qn_ref, ksq_ref, *, stream_len):
    xb = x_ref[...].astype(BF16)
    z = jnp.dot(xb, w1_ref[...], preferred_element_type=F32)
    o1 = Q_LORA
    o2 = o1 + KV_LORA
    o3 = o2 + POOL_WIDTH
    o4 = o3 + QK_ROPE
    zq, zkv, u = z[:, :o1], z[:, o1:o2], z[:, o2:o3]
    zkr, zkr_rot = z[:, o3:o4], z[:, o4:o4 + QK_ROPE]
    u_ref[0] = u

    c = zkv * lax.rsqrt(jnp.mean(zkv * zkv, axis=-1, keepdims=True) + RMS_EPS) * kvg_ref[...]
    ckv_ref[0] = c
    kr = zkr * cosr_ref[...] + zkr_rot * sinr_ref[...]
    kr_ref[0] = kr
    cb = c.astype(BF16)
    krb = kr.astype(BF16)
    kcat_ref[0, :, 0:KV_LORA] = cb
    kcat_ref[0, :, KV_LORA:QK_CAT] = krb
    cf = cb.astype(F32)
    krf = krb.astype(F32)
    k_sq = jnp.sum(cf * cf, axis=-1, keepdims=True) + jnp.sum(krf * krf, axis=-1, keepdims=True)
    ksq_ref[0, 0] = jnp.broadcast_to(jnp.max(k_sq, axis=0, keepdims=True), ksq_ref.shape[2:])
    v_t = jnp.dot(wuv_ref[...], c.T.astype(BF16), preferred_element_type=F32)
    ones_row = lax.broadcasted_iota(jnp.int32, (VH_ROWS - V_HEAD, c.shape[0]), 0) == 0
    ones_rows = jnp.where(ones_row, 1.0, 0.0).astype(BF16)
    for h in range(N_HEADS):
        vT_ref[0, 0, h, 0:V_HEAD, :] = v_t[h * V_HEAD:(h + 1) * V_HEAD].astype(BF16)
        vT_ref[0, 0, h, V_HEAD:VH_ROWS, :] = ones_rows

    zqn = zq * lax.rsqrt(jnp.mean(zq * zq, axis=-1, keepdims=True) + RMS_EPS) * (qg_ref[...] * Q_SCALE)
    zqn_t = zqn.T.astype(BF16)
    q_all = jnp.dot(wq_ref[...], zqn_t, preferred_element_type=F32)
    cos_t = cost_ref[...]
    sin_t = sint_ref[...]
    for h in range(N_HEADS):
        b = h * Q_ROWS
        lat = q_all[b:b + KV_LORA].astype(BF16)
        rope = (q_all[b + KV_LORA:b + KV_LORA + QK_ROPE] * cos_t
                + q_all[b + KV_LORA + QK_ROPE:b + Q_ROWS] * sin_t).astype(BF16)
        if stream_len is None:
            qT_ref[0, h, 0:KV_LORA, :] = lat
            qT_ref[0, h, KV_LORA:QK_CAT, :] = rope
        else:
            for s in range(lat.shape[1] // stream_len):
                src = slice(s * stream_len, (s + 1) * stream_len)
                dst = slice(h * stream_len, (h + 1) * stream_len)
                qT_ref[s, 0:KV_LORA, dst] = lat[:, src]
                qT_ref[s, KV_LORA:QK_CAT, dst] = rope[:, src]
        latf = lat.astype(F32)
        ropef = rope.astype(F32)
        q_sq = jnp.sum(latf * latf, axis=0, keepdims=True) + jnp.sum(ropef * ropef, axis=0, keepdims=True)
        qn_ref[0, h:h + 1, :] = jnp.sqrt(q_sq)


def _mixer_in(x, pos, w1, wq, wuv, qg, kvg, stream_len=None):
    bsz, t, d = x.shape
    tm = TOKEN_TILE
    nt = t // tm
    if stream_len is None:
        q_spec = pl.BlockSpec((1, N_HEADS, QK_CAT, tm), lambda b, i: (b, 0, 0, i))
        q_shape = jax.ShapeDtypeStruct((bsz, N_HEADS, QK_CAT, t), BF16)
    else:
        per_tile = tm // stream_len
        q_spec = pl.BlockSpec((per_tile, QK_CAT, N_HEADS * stream_len), lambda b, i: (b * nt + i, 0, 0))
        q_shape = jax.ShapeDtypeStruct((bsz * t // stream_len, QK_CAT, N_HEADS * stream_len), BF16)
    half = QK_ROPE // 2
    inv = 1.0 / (ROPE_THETA ** (jnp.arange(half, dtype=F32) / half))
    ang = pos.astype(F32)[:, None] * inv[None, :]
    cos_r = jnp.tile(jnp.cos(ang), (1, 2))
    sin_r = jnp.tile(jnp.sin(ang), (1, 2))
    cos_t, sin_t = cos_r.T, sin_r.T
    n1 = w1.shape[1]
    outs = pl.pallas_call(
        functools.partial(_mixer_in_kernel, stream_len=stream_len),
        grid=(bsz, nt),
        in_specs=[
            pl.BlockSpec((None, tm, d), lambda b, i: (b, i, 0)),
            _const_spec((d, n1)),
            _const_spec((N_HEADS * Q_ROWS, Q_LORA)),
            _const_spec((N_HEADS * V_HEAD, KV_LORA)),
            _const_spec((1, Q_LORA)),
            _const_spec((1, KV_LORA)),
            pl.BlockSpec((tm, QK_ROPE), lambda b, i: (i, 0)),
            pl.BlockSpec((tm, QK_ROPE), lambda b, i: (i, 0)),
            pl.BlockSpec((QK_ROPE, tm), lambda b, i: (0, i)),
            pl.BlockSpec((QK_ROPE, tm), lambda b, i: (0, i)),
        ],
        out_specs=[
            q_spec,
            pl.BlockSpec((1, tm, QK_CAT), lambda b, i: (b, i, 0)),
            pl.BlockSpec((1, 1, N_HEADS, VH_ROWS, tm), lambda b, i: (b, i, 0, 0, 0)),
            pl.BlockSpec((1, tm, KV_LORA), lambda b, i: (b, i, 0)),
            pl.BlockSpec((1, tm, QK_ROPE), lambda b, i: (b, i, 0)),
            pl.BlockSpec((1, tm, POOL_WIDTH), lambda b, i: (b, i, 0)),
            pl.BlockSpec((1, N_HEADS, tm), lambda b, i: (b, 0, i)),
            pl.BlockSpec((1, 1, SUBLANES, LANES), lambda b, i: (b, i, 0, 0)),
        ],
        out_shape=[
            q_shape,
            jax.ShapeDtypeStruct((bsz, t, QK_CAT), BF16),
            jax.ShapeDtypeStruct((bsz, nt, N_HEADS, VH_ROWS, tm), BF16),
            jax.ShapeDtypeStruct((bsz, t, KV_LORA), F32),
            jax.ShapeDtypeStruct((bsz, t, QK_ROPE), F32),
            jax.ShapeDtypeStruct((bsz, t, POOL_WIDTH), F32),
            jax.ShapeDtypeStruct((bsz, N_HEADS, t), F32),
            jax.ShapeDtypeStruct((bsz, nt, SUBLANES, LANES), F32),
        ],
        compiler_params=_params("parallel", "parallel"),
        name="mixer_in",
    )(x, w1, wq, wuv, qg, kvg, cos_r, sin_r, cos_t, sin_t)
    return outs


def _attention_kernel(kmax_ref, qT_ref, qn_ref, kcat_ref, vT_ref, oT_ref, m_ref, alpha_ref, acc_ref,
                      s_ref, mt_ref, p_ref, *, tq, tk, q_offset, n_keys, n_tiles):
    den = V_HEAD

    def values(j, h):
        return vT_ref[0, j, h]

    i = pl.program_id(1)
    q0 = q_offset + i * tq
    first_end = (q0 // CHUNK + 1) * CHUNK
    last_end = ((q0 + tq - 1) // CHUNK + 1) * CHUNK
    n_all = (jnp.minimum(last_end, n_keys) + tk - 1) // tk
    n_full = jnp.minimum(first_end // tk, n_all)
    last = N_HEADS - 1

    def scores(j, h):
        k = kcat_ref[0, pl.ds(pl.multiple_of(j * tk, tk), tk), :]
        return jnp.dot(k, qT_ref[0, h], preferred_element_type=F32)

    def visible(j):
        kchunk = lax.shift_right_arithmetic(j * tk + lax.broadcasted_iota(jnp.int32, (tk, 1), 0), CHUNK_SHIFT)
        qchunk = lax.shift_right_arithmetic(q0 + lax.broadcasted_iota(jnp.int32, (1, tq), 1), CHUNK_SHIFT)
        return kchunk <= qchunk

    def run_tiles(first_stage, tile_body, final_stage, tiles_per_trip=1):
        def loop(lo, hi, masked, per_trip=1):
            def body(t, carry):
                for r in range(per_trip):
                    tile_body(per_trip * t + r, masked)
                return carry

            lax.fori_loop(lo, hi, body, 0)

        first_stage(0, False)
        n_grouped = (n_full // tiles_per_trip) * tiles_per_trip
        loop(0, n_full // tiles_per_trip, False, per_trip=tiles_per_trip)
        loop(n_grouped, n_full, False)
        first_stage(jnp.minimum(n_full, n_tiles - 1), True)
        loop(n_full, n_all, True)
        if final_stage is not None:
            final_stage(n_all - 1)

    def stage_p(j, h, slot, masked):
        x = scores(j, h) - qn_ref[0, h:h + 1, :] * kmax_ref[pl.program_id(0)]
        if masked:
            x = jnp.where(visible(j), x, NEG_INF)
        p_ref[slot] = jnp.exp2(x).astype(BF16)

    def stage_v(j, h, slot):
        acc_ref[h] += jnp.dot(values(j, h), p_ref[slot], preferred_element_type=F32)

    def bounded_tile(j, masked):
        j_next = jnp.minimum(j + 1, n_tiles - 1)
        for h in range(N_HEADS):
            if h < last:
                stage_p(j, h + 1, (h + 1) % 2, masked)
            else:
                stage_p(j_next, 0, 0, masked)
            stage_v(j, h, h % 2)

    def bounded_path():
        acc_ref[...] = jnp.zeros(acc_ref.shape, F32)
        run_tiles(lambda j, masked: stage_p(j, 0, 0, masked), bounded_tile, None,
                  tiles_per_trip=KV_TILES_PER_TRIP)

    def stage_a(j, h, slot):
        s = scores(j, h)
        s_ref[slot] = s
        mt_ref[slot] = jnp.max(s, axis=0, keepdims=True)

    def stage_b(j, h, slot, masked):
        s = s_ref[slot]
        mt = mt_ref[slot]
        if masked:
            s = jnp.where(visible(j), s, NEG_INF)
            mt = jnp.max(s, axis=0, keepdims=True)
        m_old = m_ref[h]
        m_new = jnp.maximum(m_old, mt)
        m_ref[h] = m_new
        alpha_ref[h] = jnp.exp2(m_old - m_new)
        p_ref[slot] = jnp.exp2(s - m_new).astype(BF16)

    def stage_c(j, h, slot):
        pv = jnp.dot(values(j, h), p_ref[slot], preferred_element_type=F32)
        acc_ref[h] = alpha_ref[h] * acc_ref[h] + pv

    def exact_tile(j, masked):
        j_next = jnp.minimum(j + 1, n_tiles - 1)
        j_prev = jnp.maximum(j - 1, 0)
        for h in range(N_HEADS):
            if h < last:
                stage_a(j, h + 1, (h + 1) % 2)
            else:
                stage_a(j_next, 0, 0)
            stage_b(j, h, h % 2, masked)
            if h > 0:
                stage_c(j, h - 1, (h - 1) % 2)
            else:
                stage_c(j_prev, last, last % 2)

    def exact_path():
        m_ref[...] = jnp.full(m_ref.shape, NEG_INF, F32)
        acc_ref[...] = jnp.zeros(acc_ref.shape, F32)
        p_ref[last % 2] = jnp.zeros((tk, tq), BF16)
        alpha_ref[last] = jnp.ones((1, tq), F32)
        run_tiles(lambda j, masked: stage_a(j, 0, 0), exact_tile,
                  lambda j: stage_c(j, last, last % 2))

    bounded_path()
    denom = jnp.concatenate([acc_ref[h, den:den + 1, :] for h in range(N_HEADS)], axis=0)
    pl.when(jnp.logical_not(jnp.min(denom) >= L_MIN))(exact_path)

    for h in range(N_HEADS):
        o_t = acc_ref[h, 0:den, :] / acc_ref[h, den:den + 1, :]
        oT_ref[0, h * V_HEAD:(h + 1) * V_HEAD, :] = o_t.astype(BF16)


def _attention(qT, qn, kmax, kcat, vT, *, tq, tk, q_offset, n_keys):
    bsz, _, _, t_q = qT.shape
    t_k = kcat.shape[1]
    n_tiles = t_k // tk
    kern = functools.partial(_attention_kernel, tq=tq, tk=tk, q_offset=q_offset, n_keys=n_keys,
                             n_tiles=n_tiles)
    resident = pl.Buffered(1)
    return pl.pallas_call(
        kern,
        grid=(bsz, t_q // tq),
        in_specs=[
            pl.BlockSpec(memory_space=pltpu.SMEM),
            pl.BlockSpec((1, N_HEADS, QK_CAT, tq), lambda b, i: (b, 0, 0, i)),
            pl.BlockSpec((1, N_HEADS, tq), lambda b, i: (b, 0, i)),
            pl.BlockSpec((1, t_k, QK_CAT), lambda b, i: (b, 0, 0), pipeline_mode=resident),
            pl.BlockSpec((1, n_tiles, N_HEADS, VH_ROWS, tk), lambda b, i: (b, 0, 0, 0, 0),
                         pipeline_mode=resident),
        ],
        out_specs=pl.BlockSpec((1, N_HEADS * V_HEAD, tq), lambda b, i: (b, 0, i)),
        out_shape=jax.ShapeDtypeStruct((bsz, N_HEADS * V_HEAD, t_q), BF16),
        scratch_shapes=[
            pltpu.VMEM((N_HEADS, 1, tq), F32),
            pltpu.VMEM((N_HEADS, 1, tq), F32),
            pltpu.VMEM((N_HEADS, VH_ROWS, tq), F32),
            pltpu.VMEM((2, tk, tq), F32),
            pltpu.VMEM((2, 1, tq), F32),
            pltpu.VMEM((2, tk, tq), BF16),
        ],
        compiler_params=_params("parallel", "arbitrary"),
        name="attention",
    )(kmax, qT, qn, kcat, vT)


def _sample_attention_kernel(qT_ref, cc_ref, ckr_ref, cn_ref, krn_ref, wuvT_ref, oT_ref, *, n_s, past):
    lanes = N_HEADS * n_s
    q_t = qT_ref[0]
    cc = cc_ref[0].astype(BF16)
    cn = cn_ref[0].astype(BF16)
    k_cached = jnp.concatenate([cc, ckr_ref[0].astype(BF16)], axis=1)
    k_new = jnp.concatenate([cn, krn_ref[0].astype(BF16)], axis=1)
    s_c = jnp.dot(k_cached, q_t, preferred_element_type=F32)
    s_n = jnp.dot(k_new, q_t, preferred_element_type=F32)

    q_idx = lax.rem(lax.broadcasted_iota(jnp.int32, (1, lanes), 1), n_s)
    qchunk = lax.shift_right_arithmetic(past + q_idx, CHUNK_SHIFT)
    kchunk_c = lax.shift_right_arithmetic(lax.broadcasted_iota(jnp.int32, (past, 1), 0), CHUNK_SHIFT)
    kchunk_n = lax.shift_right_arithmetic(past + lax.broadcasted_iota(jnp.int32, (n_s, 1), 0), CHUNK_SHIFT)
    s_c = jnp.where(kchunk_c <= qchunk, s_c, NEG_INF)
    s_n = jnp.where(kchunk_n <= qchunk, s_n, NEG_INF)

    m = jnp.maximum(jnp.max(s_c, axis=0, keepdims=True), jnp.max(s_n, axis=0, keepdims=True))
    p_c = jnp.exp2(s_c - m)
    p_n = jnp.exp2(s_n - m)
    denom = jnp.sum(p_c, axis=0, keepdims=True) + jnp.sum(p_n, axis=0, keepdims=True)
    contract_rows = (((0,), (0,)), ((), ()))
    att_t = (lax.dot_general(cc, p_c.astype(BF16), contract_rows, preferred_element_type=F32)
             + lax.dot_general(cn, p_n.astype(BF16), contract_rows, preferred_element_type=F32)) / denom
    att_t = att_t.astype(BF16)
    for h in range(N_HEADS):
        o_t = jnp.dot(wuvT_ref[h], att_t[:, h * n_s:(h + 1) * n_s], preferred_element_type=F32)
        oT_ref[0, h * V_HEAD:(h + 1) * V_HEAD, :] = o_t.astype(BF16)


def _sample_attention(qT, cache_c, cache_kr, c_new, kr_new, wuvT):
    bsz, past, _ = cache_c.shape
    n_s = c_new.shape[1]
    kern = functools.partial(_sample_attention_kernel, n_s=n_s, past=past)
    return pl.pallas_call(
        kern,
        grid=(bsz,),
        in_specs=[
            pl.BlockSpec((1, QK_CAT, N_HEADS * n_s), lambda b: (b, 0, 0)),
            pl.BlockSpec((1, past, KV_LORA), lambda b: (b, 0, 0)),
            pl.BlockSpec((1, past, QK_ROPE), lambda b: (b, 0, 0)),
            pl.BlockSpec((1, n_s, KV_LORA), lambda b: (b, 0, 0)),
            pl.BlockSpec((1, n_s, QK_ROPE), lambda b: (b, 0, 0)),
            _const_spec((N_HEADS, V_HEAD, KV_LORA)),
        ],
        out_specs=pl.BlockSpec((1, N_HEADS * V_HEAD, n_s), lambda b: (b, 0, 0)),
        out_shape=jax.ShapeDtypeStruct((bsz, N_HEADS * V_HEAD, n_s), BF16),
        compiler_params=_params("parallel"),
        name="sample_attention",
    )(qT, cache_c, cache_kr, c_new, kr_new, wuvT)


def _layer_norm(r, g, b):
    mu = jnp.mean(r, axis=-1, keepdims=True)
    cen = r - mu
    var = jnp.mean(cen * cen, axis=-1, keepdims=True)
    return cen * lax.rsqrt(var + LN_EPS) * g + b


def _mix_kernel(x_ref, oT_ref, u_ref, pre_ref, wg_ref, bg_ref, wup_ref, wpool_ref, pscale_ref, wo_ref,
                g1_ref, b1_ref, h_ref, ext_ref, d_ref, *, seg_len, n_seg, pos0_fn, alpha):
    d_model = x_ref.shape[-1]
    x = x_ref[...]
    xb = x.astype(BF16)
    zg = jnp.dot(xb, wg_ref[...], preferred_element_type=F32) + bg_ref[...]
    gates = 1.0 / (1.0 + jnp.exp(-zg))

    pos0 = pos0_fn(pl.program_id(0))
    row = lax.broadcasted_iota(jnp.int32, (seg_len, 1), 0)
    for s in range(n_seg):
        ext_ref[0:POOL_HALO, :] = pre_ref[s]
        ext_ref[POOL_HALO:POOL_HALO + seg_len, :] = u_ref[s * seg_len:(s + 1) * seg_len, :]
        for g, w in enumerate(POOL_WINDOWS):
            lanes = slice(g * POOL_GROUP_DIM, (g + 1) * POOL_GROUP_DIM)
            cur = ext_ref[POOL_HALO:POOL_HALO + seg_len, lanes]
            acc = cur
            for k in range(1, w):
                acc = acc + ext_ref[POOL_HALO - k:POOL_HALO - k + seg_len, lanes]
            cnt = jnp.minimum(pos0 + row + 1, w).astype(F32)
            d_ref[s * seg_len:(s + 1) * seg_len, lanes] = (acc * (1.0 / cnt) - cur).astype(BF16)

    pool_out = d_model // POOL_GROUPS
    parts = []
    for g in range(POOL_GROUPS):
        dg = d_ref[:, g * POOL_GROUP_DIM:(g + 1) * POOL_GROUP_DIM]
        parts.append(jnp.dot(dg, wpool_ref[g], preferred_element_type=F32))
    p_branch = jnp.concatenate(parts, axis=-1) * pscale_ref[...]

    wup = wup_ref[...]
    a_branch = jnp.concatenate(
        [lax.dot_general(oT_ref[s], wup, (((0,), (0,)), ((), ())), preferred_element_type=F32)
         for s in range(n_seg)], axis=0)
    m = gates[:, :d_model] * a_branch + gates[:, d_model:] * p_branch
    r = alpha * x + jnp.dot(m.astype(BF16), wo_ref[...], preferred_element_type=F32)
    h_ref[...] = _layer_norm(r, g1_ref[...], b1_ref[...])


def _mix(x2, oT, u2, prefix, wg, bg, wup, wpool, pscale, wo, g1, b1, *, seg_len, pos0_fn, alpha):
    n, d = x2.shape
    tm = TOKEN_TILE
    n_seg = tm // seg_len
    _, hv, t_o = oT.shape
    if n_seg == 1:
        tiles_per_row = t_o // tm
        o_spec = pl.BlockSpec((1, hv, tm), lambda i: (i // tiles_per_row, 0, i % tiles_per_row))
    else:
        assert t_o == seg_len
        o_spec = pl.BlockSpec((n_seg, hv, seg_len), lambda i: (i, 0, 0))
    kern = functools.partial(_mix_kernel, seg_len=seg_len, n_seg=n_seg, pos0_fn=pos0_fn, alpha=alpha)
    return pl.pallas_call(
        kern,
        grid=(n // tm,),
        in_specs=[
            pl.BlockSpec((tm, d), lambda i: (i, 0)),
            o_spec,
            pl.BlockSpec((tm, POOL_WIDTH), lambda i: (i, 0)),
            pl.BlockSpec((n_seg, POOL_HALO, POOL_WIDTH), lambda i: (i, 0, 0)),
            _const_spec(wg.shape), _const_spec(bg.shape), _const_spec(wup.shape),
            _const_spec(wpool.shape), _const_spec(pscale.shape), _const_spec(wo.shape),
            _const_spec(g1.shape), _const_spec(b1.shape),
        ],
        out_specs=pl.BlockSpec((tm, d), lambda i: (i, 0)),
        out_shape=jax.ShapeDtypeStruct((n, d), F32),
        scratch_shapes=[
            pltpu.VMEM((POOL_HALO + seg_len, POOL_WIDTH), F32),
            pltpu.VMEM((tm, POOL_WIDTH), BF16),
        ],
        compiler_params=_params("parallel"),
        name="mix",
    )(x2, oT, u2, prefix, wg, bg, wup, wpool, pscale, wo, g1, b1)


def _ffn_kernel(h_ref, wgate_ref, wupf_ref, wdown_ref, g2_ref, b2_ref, y_ref, *, alpha):
    h = h_ref[...]
    hb = h.astype(BF16)
    f = None
    lo = 0
    for width in FF_CHUNKS:
        gate = jnp.dot(hb, wgate_ref[:, lo:lo + width], preferred_element_type=F32)
        up = jnp.dot(hb, wupf_ref[:, lo:lo + width], preferred_element_type=F32)
        act = (gate / (1.0 + jnp.exp(-gate)) * up).astype(BF16)
        part = jnp.dot(act, wdown_ref[lo:lo + width, :], preferred_element_type=F32)
        f = part if f is None else f + part
        lo += width
    y_ref[...] = _layer_norm(alpha * h + f, g2_ref[...], b2_ref[...])


def _ffn(h2, wgate, wupf, wdown, g2, b2, *, alpha):
    n, d = h2.shape
    tm = TOKEN_TILE
    assert sum(FF_CHUNKS) == wgate.shape[1]
    return pl.pallas_call(
        functools.partial(_ffn_kernel, alpha=alpha),
        grid=(n // tm,),
        in_specs=[
            pl.BlockSpec((tm, d), lambda i: (i, 0)),
            _const_spec(wgate.shape), _const_spec(wupf.shape), _const_spec(wdown.shape),
            _const_spec(g2.shape), _const_spec(b2.shape),
        ],
        out_specs=pl.BlockSpec((tm, d), lambda i: (i, 0)),
        out_shape=jax.ShapeDtypeStruct((n, d), F32),
        compiler_params=_params("parallel"),
        name="ffn",
    )(h2, wgate, wupf, wdown, g2, b2)


def _prompt_pos0(tiles_per_seq, i):
    return (i % tiles_per_seq) * TOKEN_TILE


def _sample_pos0(past, i):
    return past


def kernel(x_prompt, x_sample, cache_kv_latent, cache_k_rope, state_pool, w_in, b_gate, q_norm_g, w_uq,
           kv_norm_g, w_uk, w_uv, w_attn_up, w_pool, pool_scale, w_o, ln1_g, ln1_b, w_gate, w_up, w_down,
           ln2_g, ln2_b):
    depth = w_in.shape[0]
    bp, n_p, d = x_prompt.shape
    bs, n_s, _ = x_sample.shape
    past = cache_kv_latent.shape[2]
    alpha = (2 * depth) ** 0.25
    half = QK_ROPE // 2
    o1 = Q_LORA
    o2 = o1 + KV_LORA
    o3 = o2 + QK_ROPE
    o4 = o3 + POOL_WIDTH

    pos_p = jnp.arange(n_p, dtype=jnp.int32)
    pos_s = past + (jnp.arange(bs * n_s, dtype=jnp.int32) % n_s)

    h_p, h_s = x_prompt, x_sample
    c_p_list, kr_p_list, pool_p_list = [], [], []
    c_s_list, kr_s_list, pool_s_list = [], [], []
    for l in range(depth):
        wl = w_in[l]
        w_kr = wl[:, o2:o3]
        w_kr_rot = jnp.concatenate([-w_kr[:, half:], w_kr[:, :half]], axis=1)
        w1 = jnp.concatenate([wl[:, :o2], wl[:, o3:o4], w_kr, w_kr_rot], axis=1).astype(BF16)
        wg = wl[:, o4:].astype(BF16)
        bg = b_gate[l][None, :]
        wq = _fold_q(w_uq[l], w_uk[l])
        qg = q_norm_g[l][None, :]
        kvg = kv_norm_g[l][None, :]
        wuvT = jnp.transpose(w_uv[l], (1, 2, 0)).astype(BF16)
        wuv = wuvT.reshape(N_HEADS * V_HEAD, KV_LORA)
        wup = w_attn_up[l].astype(BF16)
        wpool = w_pool[l].astype(BF16)
        pscale = pool_scale[l][None, :]
        wo = w_o[l].astype(BF16)
        g1, b1 = ln1_g[l][None, :], ln1_b[l][None, :]
        g2, b2 = ln2_g[l][None, :], ln2_b[l][None, :]
        wgate, wupf, wdown = w_gate[l].astype(BF16), w_up[l].astype(BF16), w_down[l].astype(BF16)

        qT, kcat, vT, c, kr, u, qn, ksq = _mixer_in(h_p, pos_p, w1, wq, wuv, qg, kvg)
        kmax = jnp.sqrt(jnp.max(ksq[:, :, 0, 0], axis=1))
        oT = _attention(qT, qn, kmax, kcat, vT, tq=Q_TILE, tk=KV_TILE, q_offset=0, n_keys=n_p)
        tiles_per_seq = n_p // TOKEN_TILE
        tails = u.reshape(bp, tiles_per_seq, TOKEN_TILE, POOL_WIDTH)[:, :, TOKEN_TILE - POOL_HALO:, :]
        prefix = jnp.concatenate([jnp.zeros_like(tails[:, :1]), tails[:, :-1]], axis=1)
        prefix = prefix.reshape(bp * tiles_per_seq, POOL_HALO, POOL_WIDTH)
        hmid = _mix(h_p.reshape(bp * n_p, d), oT, u.reshape(bp * n_p, POOL_WIDTH), prefix,
                    wg, bg, wup, wpool, pscale, wo, g1, b1, seg_len=TOKEN_TILE,
                    pos0_fn=functools.partial(_prompt_pos0, tiles_per_seq), alpha=alpha)
        y = _ffn(hmid, wgate, wupf, wdown, g2, b2, alpha=alpha)
        c_p_list.append(c)
        kr_p_list.append(kr)
        pool_p_list.append(u[:, n_p - POOL_STATE:])
        h_p = y.reshape(bp, n_p, d)

        xs = h_s.reshape(1, bs * n_s, d)
        qT, _, _, c, kr, u, _, _ = _mixer_in(xs, pos_s, w1, wq, wuv, qg, kvg, stream_len=n_s)
        c = c.reshape(bs, n_s, KV_LORA)
        kr = kr.reshape(bs, n_s, QK_ROPE)
        u = u.reshape(bs, n_s, POOL_WIDTH)
        oT = _sample_attention(qT, cache_kv_latent[l], cache_k_rope[l], c, kr, wuvT)
        prefix = jnp.pad(state_pool[l], ((0, 0), (POOL_HALO - POOL_STATE, 0), (0, 0)))
        hmid = _mix(h_s.reshape(bs * n_s, d), oT, u.reshape(bs * n_s, POOL_WIDTH), prefix,
                    wg, bg, wup, wpool, pscale, wo, g1, b1, seg_len=n_s,
                    pos0_fn=functools.partial(_sample_pos0, past), alpha=alpha)
        y = _ffn(hmid, wgate, wupf, wdown, g2, b2, alpha=alpha)
        u_all = jnp.concatenate([state_pool[l], u], axis=1)
        c_s_list.append(c)
        kr_s_list.append(kr)
        pool_s_list.append(u_all[:, u_all.shape[1] - POOL_STATE:])
        h_s = y.reshape(bs, n_s, d)

    def stack(parts):
        return parts[0][None] if len(parts) == 1 else jnp.stack(parts)

    return (h_p, h_s, stack(c_p_list), stack(kr_p_list), stack(pool_p_list),
            stack(c_s_list), stack(kr_s_list), stack(pool_s_list))
```

```python
import functools
import math

import jax
import jax.numpy as jnp
from jax import lax
from jax.experimental import pallas as pl
from jax.experimental.pallas import tpu as pltpu

F32 = jnp.float32
BF16 = jnp.bfloat16

LANES = 128
SUBLANES = 8
BF16_SUBLANES = 16

N_HEADS = 8
QK_NOPE = 64
QK_ROPE = 32
QK_HEAD = QK_NOPE + QK_ROPE
V_HEAD = 64
Q_LORA = 256
KV_LORA = 128
QK_CAT = KV_LORA + QK_ROPE
Q_ROWS = KV_LORA + 2 * QK_ROPE
VH_ROWS = V_HEAD + BF16_SUBLANES
CHUNK = 64
CHUNK_SHIFT = CHUNK.bit_length() - 1
ROPE_THETA = 10000.0
ATTN_SCALE = QK_HEAD ** -0.5
POOL_WIDTH = 512
POOL_GROUPS = 4
POOL_GROUP_DIM = POOL_WIDTH // POOL_GROUPS
POOL_WINDOWS = (2, 4, 8, 16)
POOL_STATE = max(POOL_WINDOWS) - 1
POOL_HALO = POOL_STATE + 1
RMS_EPS = 1e-6
LN_EPS = 1e-5
NEG_INF = -1e30
L_MIN = 2.0 ** -60
Q_SCALE = ATTN_SCALE * math.log2(math.e)

VMEM_LIMIT_BYTES = 56 * 1024 * 1024

TOKEN_TILE = 512
KV_TILE = 512
Q_TILE = 512
KV_TILES_PER_TRIP = 4
FF_CHUNKS = (768, 768, 768, 512)


def _const_spec(shape):
    nd = len(shape)
    return pl.BlockSpec(shape, lambda *_: (0,) * nd, pipeline_mode=pl.Buffered(1))


def _params(*sem):
    return pltpu.CompilerParams(dimension_semantics=sem, vmem_limit_bytes=VMEM_LIMIT_BYTES)


def _fold_q_kernel(wuk_ref, wuqn_ref, wuqr_ref, out_ref):
    half = QK_ROPE // 2
    for h in range(N_HEADS):
        lat = jnp.dot(wuk_ref[h], wuqn_ref[h], preferred_element_type=F32,
                      precision=lax.Precision.HIGHEST)
        rope = wuqr_ref[h]
        out_ref[h, 0:KV_LORA, :] = lat.astype(BF16)
        out_ref[h, KV_LORA:KV_LORA + QK_ROPE, :] = rope.astype(BF16)
        out_ref[h, KV_LORA + QK_ROPE:KV_LORA + QK_ROPE + half, :] = (-rope[half:]).astype(BF16)
        out_ref[h, KV_LORA + QK_ROPE + half:Q_ROWS, :] = rope[:half].astype(BF16)


def _fold_q(w_uq, w_uk):
    wq = w_uq.reshape(Q_LORA, N_HEADS, QK_HEAD)
    wuqn_t = jnp.transpose(wq[:, :, :QK_NOPE], (1, 2, 0))
    wuqr_t = jnp.transpose(wq[:, :, QK_NOPE:], (1, 2, 0))
    wuk_t = jnp.transpose(w_uk, (1, 0, 2))
    out = pl.pallas_call(
        _fold_q_kernel,
        out_shape=jax.ShapeDtypeStruct((N_HEADS, Q_ROWS, Q_LORA), BF16),
        name="fold_q",
    )(wuk_t, wuqn_t, wuqr_t)
    return out.reshape(N_HEADS * Q_ROWS, Q_LORA)


def _mixer_in_kernel(x_ref, w1_ref, wq_ref, wuv_ref, qg_ref, kvg_ref, cosr_ref, sinr_ref, cost_ref, sint_ref,
                     qT_ref, kcat_ref, vT_ref, ckv_ref, kr_ref, u_ref, qn_ref, ksq_ref, *, stream_len):
    xb = x_ref[...].astype(BF16)
    z = jnp.dot(xb, w1_ref[...], preferred_element_type=F32)
    o1 = Q_LORA
    o2 = o1 + KV_LORA
    o3 = o2 + POOL_WIDTH
    o4 = o3 + QK_ROPE
    zq, zkv, u = z[:, :o1], z[:, o1:o2], z[:, o2:o3]
    zkr, zkr_rot = z[:, o3:o4], z[:, o4:o4 + QK_ROPE]
    u_ref[0] = u

    c = zkv * lax.rsqrt(jnp.mean(zkv * zkv, axis=-1, keepdims=True) + RMS_EPS) * kvg_ref[...]
    ckv_ref[0] = c
    kr = zkr * cosr_ref[...] + zkr_rot * sinr_ref[...]
    kr_ref[0] = kr
    cb = c.astype(BF16)
    krb = kr.astype(BF16)
    kcat_ref[0, :, 0:KV_LORA] = cb
    kcat_ref[0, :, KV_LORA:QK_CAT] = krb
    cf = cb.astype(F32)
    krf = krb.astype(F32)
    k_sq = jnp.sum(cf * cf, axis=-1, keepdims=True) + jnp.sum(krf * krf, axis=-1, keepdims=True)
    ksq_ref[0, 0] = jnp.broadcast_to(jnp.max(k_sq, axis=0, keepdims=True), ksq_ref.shape[2:])
    v_t = jnp.dot(wuv_ref[...], c.T.astype(BF16), preferred_element_type=F32)
    ones_row = lax.broadcasted_iota(jnp.int32, (VH_ROWS - V_HEAD, c.shape[0]), 0) == 0
    ones_rows = jnp.where(ones_row, 1.0, 0.0).astype(BF16)
    for h in range(N_HEADS):
        vT_ref[0, 0, h, 0:V_HEAD, :] = v_t[h * V_HEAD:(h + 1) * V_HEAD].astype(BF16)
        vT_ref[0, 0, h, V_HEAD:VH_ROWS, :] = ones_rows

    zqn = zq * lax.rsqrt(jnp.mean(zq * zq, axis=-1, keepdims=True) + RMS_EPS) * (qg_ref[...] * Q_SCALE)
    zqn_t = zqn.T.astype(BF16)
    q_all = jnp.dot(wq_ref[...], zqn_t, preferred_element_type=F32)
    cos_t = cost_ref[...]
    sin_t = sint_ref[...]
    for h in range(N_HEADS):
        b = h * Q_ROWS
        lat = q_all[b:b + KV_LORA].astype(BF16)
        rope = (q_all[b + KV_LORA:b + KV_LORA + QK_ROPE] * cos_t
                + q_all[b + KV_LORA + QK_ROPE:b + Q_ROWS] * sin_t).astype(BF16)
        if stream_len is None:
            qT_ref[0, h, 0:KV_LORA, :] = lat
            qT_ref[0, h, KV_LORA:QK_CAT, :] = rope
        else:
            for s in range(lat.shape[1] // stream_len):
                src = slice(s * stream_len, (s + 1) * stream_len)
                dst = slice(h * stream_len, (h + 1) * stream_len)
                qT_ref[s, 0:KV_LORA, dst] = lat[:, src]
                qT_ref[s, KV_LORA:QK_CAT, dst] = rope[:, src]
        latf = lat.astype(F32)
        ropef = rope.astype(F32)
        q_sq = jnp.sum(latf * latf, axis=0, keepdims=True) + jnp.sum(ropef * ropef, axis=0, keepdims=True)
        qn_ref[0, h:h + 1, :] = jnp.sqrt(q_sq)


def _mixer_in(x, pos, w1, wq, wuv, qg, kvg, stream_len=None):
    bsz, t, d = x.shape
    tm = TOKEN_TILE
    nt = t // tm
    if stream_len is None:
        q_spec = pl.BlockSpec((1, N_HEADS, QK_CAT, tm), lambda b, i: (b, 0, 0, i))
        q_shape = jax.ShapeDtypeStruct((bsz, N_HEADS, QK_CAT, t), BF16)
    else:
        per_tile = tm // stream_len
        q_spec = pl.BlockSpec((per_tile, QK_CAT, N_HEADS * stream_len), lambda b, i: (b * nt + i, 0, 0))
        q_shape = jax.ShapeDtypeStruct((bsz * t // stream_len, QK_CAT, N_HEADS * stream_len), BF16)
    half = QK_ROPE // 2
    inv = 1.0 / (ROPE_THETA ** (jnp.arange(half, dtype=F32) / half))
    ang = pos.astype(F32)[:, None] * inv[None, :]
    cos_r = jnp.tile(jnp.cos(ang), (1, 2))
    sin_r = jnp.tile(jnp.sin(ang), (1, 2))
    cos_t, sin_t = cos_r.T, sin_r.T
    n1 = w1.shape[1]
    outs = pl.pallas_call(
        functools.partial(_mixer_in_kernel, stream_len=stream_len),
        grid=(bsz, nt),
        in_specs=[
            pl.BlockSpec((None, tm, d), lambda b, i: (b, i, 0)),
            _const_spec((d, n1)),
            _const_spec((N_HEADS * Q_ROWS, Q_LORA)),
            _const_spec((N_HEADS * V_HEAD, KV_LORA)),
            _const_spec((1, Q_LORA)),
            _const_spec((1, KV_LORA)),
            pl.BlockSpec((tm, QK_ROPE), lambda b, i: (i, 0)),
            pl.BlockSpec((tm, QK_ROPE), lambda b, i: (i, 0)),
            pl.BlockSpec((QK_ROPE, tm), lambda b, i: (0, i)),
            pl.BlockSpec((QK_ROPE, tm), lambda b, i: (0, i)),
        ],
        out_specs=[
            q_spec,
            pl.BlockSpec((1, tm, QK_CAT), lambda b, i: (b, i, 0)),
            pl.BlockSpec((1, 1, N_HEADS, VH_ROWS, tm), lambda b, i: (b, i, 0, 0, 0)),
            pl.BlockSpec((1, tm, KV_LORA), lambda b, i: (b, i, 0)),
            pl.BlockSpec((1, tm, QK_ROPE), lambda b, i: (b, i, 0)),
            pl.BlockSpec((1, tm, POOL_WIDTH), lambda b, i: (b, i, 0)),
            pl.BlockSpec((1, N_HEADS, tm), lambda b, i: (b, 0, i)),
            pl.BlockSpec((1, 1, SUBLANES, LANES), lambda b, i: (b, i, 0, 0)),
        ],
        out_shape=[
            q_shape,
            jax.ShapeDtypeStruct((bsz, t, QK_CAT), BF16),
            jax.ShapeDtypeStruct((bsz, nt, N_HEADS, VH_ROWS, tm), BF16),
            jax.ShapeDtypeStruct((bsz, t, KV_LORA), F32),
            jax.ShapeDtypeStruct((bsz, t, QK_ROPE), F32),
            jax.ShapeDtypeStruct((bsz, t, POOL_WIDTH), F32),
            jax.ShapeDtypeStruct((bsz, N_HEADS, t), F32),
            jax.ShapeDtypeStruct((bsz, nt, SUBLANES, LANES), F32),
        ],
        compiler_params=_params("parallel", "parallel"),
        name="mixer_in",
    )(x, w1, wq, wuv, qg, kvg, cos_r, sin_r, cos_t, sin_t)
    return outs


def _attention_kernel(kmax_ref, qT_ref, qn_ref, kcat_ref, vT_ref, oT_ref, m_ref, alpha_ref, acc_ref,
                      s_ref, mt_ref, p_ref, *, tq, tk, q_offset, n_keys, n_tiles):
    den = V_HEAD

    def values(j, h):
        return vT_ref[0, j, h]

    i = pl.program_id(1)
    q0 = q_offset + i * tq
    first_end = (q0 // CHUNK + 1) * CHUNK
    last_end = ((q0 + tq - 1) // CHUNK + 1) * CHUNK
    n_all = (jnp.minimum(last_end, n_keys) + tk - 1) // tk
    n_full = jnp.minimum(first_end // tk, n_all)
    last = N_HEADS - 1

    def scores(j, h):
        k = kcat_ref[0, pl.ds(pl.multiple_of(j * tk, tk), tk), :]
        return jnp.dot(k, qT_ref[0, h], preferred_element_type=F32)

    def visible(j):
        kchunk = lax.shift_right_arithmetic(j * tk + lax.broadcasted_iota(jnp.int32, (tk, 1), 0), CHUNK_SHIFT)
        qchunk = lax.shift_right_arithmetic(q0 + lax.broadcasted_iota(jnp.int32, (1, tq), 1), CHUNK_SHIFT)
        return kchunk <= qchunk

    def run_tiles(first_stage, tile_body, final_stage, tiles_per_trip=1):
        def loop(lo, hi, masked, per_trip=1):
            def body(t, carry):
                for r in range(per_trip):
                    tile_body(per_trip * t + r, masked)
                return carry

            lax.fori_loop(lo, hi, body, 0)

        first_stage(0, False)
        n_grouped = (n_full // tiles_per_trip) * tiles_per_trip
        loop(0, n_full // tiles_per_trip, False, per_trip=tiles_per_trip)
        loop(n_grouped, n_full, False)
        first_stage(jnp.minimum(n_full, n_tiles - 1), True)
        loop(n_full, n_all, True)
        if final_stage is not None:
            final_stage(n_all - 1)

    def stage_p(j, h, slot, masked):
        x = scores(j, h) - qn_ref[0, h:h + 1, :] * kmax_ref[pl.program_id(0)]
        if masked:
            x = jnp.where(visible(j), x, NEG_INF)
        p_ref[slot] = jnp.exp2(x).astype(BF16)

    def stage_v(j, h, slot):
        acc_ref[h] += jnp.dot(values(j, h), p_ref[slot], preferred_element_type=F32)

    def bounded_tile(j, masked):
        j_next = jnp.minimum(j + 1, n_tiles - 1)
        for h in range(N_HEADS):
            if h < last:
                stage_p(j, h + 1, (h + 1) % 2, masked)
            else:
                stage_p(j_next, 0, 0, masked)
            stage_v(j, h, h % 2)

    def bounded_path():
        acc_ref[...] = jnp.zeros(acc_ref.shape, F32)
        run_tiles(lambda j, masked: stage_p(j, 0, 0, masked), bounded_tile, None,
                  tiles_per_trip=KV_TILES_PER_TRIP)

    def stage_a(j, h, slot):
        s = scores(j, h)
        s_ref[slot] = s
        mt_ref[slot] = jnp.max(s, axis=0, keepdims=True)

    def stage_b(j, h, slot, masked):
        s = s_ref[slot]
        mt = mt_ref[slot]
        if masked:
            s = jnp.where(visible(j), s, NEG_INF)
            mt = jnp.max(s, axis=0, keepdims=True)
        m_old = m_ref[h]
        m_new = jnp.maximum(m_old, mt)
        m_ref[h] = m_new
        alpha_ref[h] = jnp.exp2(m_old - m_new)
        p_ref[slot] = jnp.exp2(s - m_new).astype(BF16)

    def stage_c(j, h, slot):
        pv = jnp.dot(values(j, h), p_ref[slot], preferred_element_type=F32)
        acc_ref[h] = alpha_ref[h] * acc_ref[h] + pv

    def exact_tile(j, masked):
        j_next = jnp.minimum(j + 1, n_tiles - 1)
        j_prev = jnp.maximum(j - 1, 0)
        for h in range(N_HEADS):
            if h < last:
                stage_a(j, h + 1, (h + 1) % 2)
            else:
                stage_a(j_next, 0, 0)
            stage_b(j, h, h % 2, masked)
            if h > 0:
                stage_c(j, h - 1, (h - 1) % 2)
            else:
                stage_c(j_prev, last, last % 2)

    def exact_path():
        m_ref[...] = jnp.full(m_ref.shape, NEG_INF, F32)
        acc_ref[...] = jnp.zeros(acc_ref.shape, F32)
        p_ref[last % 2] = jnp.zeros((tk, tq), BF16)
        alpha_ref[last] = jnp.ones((1, tq), F32)
        run_tiles(lambda j, masked: stage_a(j, 0, 0), exact_tile,
                  lambda j: stage_c(j, last, last % 2))

    bounded_path()
    denom = jnp.concatenate([acc_ref[h, den:den + 1, :] for h in range(N_HEADS)], axis=0)
    pl.when(jnp.logical_not(jnp.min(denom) >= L_MIN))(exact_path)

    for h in range(N_HEADS):
        o_t = acc_ref[h, 0:den, :] / acc_ref[h, den:den + 1, :]
        oT_ref[0, h * V_HEAD:(h + 1) * V_HEAD, :] = o_t.astype(BF16)


def _attention(qT, qn, kmax, kcat, vT, *, tq, tk, q_offset, n_keys):
    bsz, _, _, t_q = qT.shape
    t_k = kcat.shape[1]
    n_tiles = t_k // tk
    kern = functools.partial(_attention_kernel, tq=tq, tk=tk, q_offset=q_offset, n_keys=n_keys,
                             n_tiles=n_tiles)
    resident = pl.Buffered(1)
    return pl.pallas_call(
        kern,
        grid=(bsz, t_q // tq),
        in_specs=[
            pl.BlockSpec(memory_space=pltpu.SMEM),
            pl.BlockSpec((1, N_HEADS, QK_CAT, tq), lambda b, i: (b, 0, 0, i)),
            pl.BlockSpec((1, N_HEADS, tq), lambda b, i: (b, 0, i)),
            pl.BlockSpec((1, t_k, QK_CAT), lambda b, i: (b, 0, 0), pipeline_mode=resident),
            pl.BlockSpec((1, n_tiles, N_HEADS, VH_ROWS, tk), lambda b, i: (b, 0, 0, 0, 0),
                         pipeline_mode=resident),
        ],
        out_specs=pl.BlockSpec((1, N_HEADS * V_HEAD, tq), lambda b, i: (b, 0, i)),
        out_shape=jax.ShapeDtypeStruct((bsz, N_HEADS * V_HEAD, t_q), BF16),
        scratch_shapes=[
            pltpu.VMEM((N_HEADS, 1, tq), F32),
            pltpu.VMEM((N_HEADS, 1, tq), F32),
            pltpu.VMEM((N_HEADS, VH_ROWS, tq), F32),
            pltpu.VMEM((2, tk, tq), F32),
            pltpu.VMEM((2, 1, tq), F32),
            pltpu.VMEM((2, tk, tq), BF16),
        ],
        compiler_params=_params("parallel", "arbitrary"),
        name="attention",
    )(kmax, qT, qn, kcat, vT)


def _sample_attention_kernel(qT_ref, cc_ref, ckr_ref, cn_ref, krn_ref, wuvT_ref, oT_ref, *, n_s, past):
    lanes = N_HEADS * n_s
    q_t = qT_ref[0]
    cc = cc_ref[0].astype(BF16)
    cn = cn_ref[0].astype(BF16)
    k_cached = jnp.concatenate([cc, ckr_ref[0].astype(BF16)], axis=1)
    k_new = jnp.concatenate([cn, krn_ref[0].astype(BF16)], axis=1)
    s_c = jnp.dot(k_cached, q_t, preferred_element_type=F32)
    s_n = jnp.dot(k_new, q_t, preferred_element_type=F32)

    q_idx = lax.rem(lax.broadcasted_iota(jnp.int32, (1, lanes), 1), n_s)
    qchunk = lax.shift_right_arithmetic(past + q_idx, CHUNK_SHIFT)
    kchunk_c = lax.shift_right_arithmetic(lax.broadcasted_iota(jnp.int32, (past, 1), 0), CHUNK_SHIFT)
    kchunk_n = lax.shift_right_arithmetic(past + lax.broadcasted_iota(jnp.int32, (n_s, 1), 0), CHUNK_SHIFT)
    s_c = jnp.where(kchunk_c <= qchunk, s_c, NEG_INF)
    s_n = jnp.where(kchunk_n <= qchunk, s_n, NEG_INF)

    m = jnp.maximum(jnp.max(s_c, axis=0, keepdims=True), jnp.max(s_n, axis=0, keepdims=True))
    p_c = jnp.exp2(s_c - m)
    p_n = jnp.exp2(s_n - m)
    denom = jnp.sum(p_c, axis=0, keepdims=True) + jnp.sum(p_n, axis=0, keepdims=True)
    contract_rows = (((0,), (0,)), ((), ()))
    att_t = (lax.dot_general(cc, p_c.astype(BF16), contract_rows, preferred_element_type=F32)
             + lax.dot_general(cn, p_n.astype(BF16), contract_rows, preferred_element_type=F32)) / denom
    att_t = att_t.astype(BF16)
    for h in range(N_HEADS):
        o_t = jnp.dot(wuvT_ref[h], att_t[:, h * n_s:(h + 1) * n_s], preferred_element_type=F32)
        oT_ref[0, h * V_HEAD:(h + 1) * V_HEAD, :] = o_t.astype(BF16)


def _sample_attention(qT, cache_c, cache_kr, c_new, kr_new, wuvT):
    bsz, past, _ = cache_c.shape
    n_s = c_new.shape[1]
    kern = functools.partial(_sample_attention_kernel, n_s=n_s, past=past)
    return pl.pallas_call(
        kern,
        grid=(bsz,),
        in_specs=[
            pl.BlockSpec((1, QK_CAT, N_HEADS * n_s), lambda b: (b, 0, 0)),
            pl.BlockSpec((1, past, KV_LORA), lambda b: (b, 0, 0)),
            pl.BlockSpec((1, past, QK_ROPE), lambda b: (b, 0, 0)),
            pl.BlockSpec((1, n_s, KV_LORA), lambda b: (b, 0, 0)),
            pl.BlockSpec((1, n_s, QK_ROPE), lambda b: (b, 0, 0)),
            _const_spec((N_HEADS, V_HEAD, KV_LORA)),
        ],
        out_specs=pl.BlockSpec((1, N_HEADS * V_HEAD, n_s), lambda b: (b, 0, 0)),
        out_shape=jax.ShapeDtypeStruct((bsz, N_HEADS * V_HEAD, n_s), BF16),
        compiler_params=_params("parallel"),
        name="sample_attention",
    )(qT, cache_c, cache_kr, c_new, kr_new, wuvT)


def _layer_norm(r, g, b):
    mu = jnp.mean(r, axis=-1, keepdims=True)
    cen = r - mu
    var = jnp.mean(cen * cen, axis=-1, keepdims=True)
    return cen * lax.rsqrt(var + LN_EPS) * g + b


def _finish_kernel(x_ref, oT_ref, u_ref, pre_ref, wg_ref, bg_ref, wup_ref, wpool_ref, pscale_ref, wo_ref,
                g1_ref, b1_ref, wgate_ref, wupf_ref, wdown_ref, g2_ref, b2_ref, y_ref, ext_ref, d_ref,
                *, seg_len, n_seg, pos0_fn, alpha):
    d_model = x_ref.shape[-1]
    x = x_ref[...]
    xb = x.astype(BF16)
    zg = jnp.dot(xb, wg_ref[...], preferred_element_type=F32) + bg_ref[...]
    gates = 1.0 / (1.0 + jnp.exp(-zg))

    pos0 = pos0_fn(pl.program_id(0))
    row = lax.broadcasted_iota(jnp.int32, (seg_len, 1), 0)
    for s in range(n_seg):
        ext_ref[0:POOL_HALO, :] = pre_ref[s]
        ext_ref[POOL_HALO:POOL_HALO + seg_len, :] = u_ref[s * seg_len:(s + 1) * seg_len, :]
        for g, w in enumerate(POOL_WINDOWS):
            lanes = slice(g * POOL_GROUP_DIM, (g + 1) * POOL_GROUP_DIM)
            cur = ext_ref[POOL_HALO:POOL_HALO + seg_len, lanes]
            acc = cur
            for k in range(1, w):
                acc = acc + ext_ref[POOL_HALO - k:POOL_HALO - k + seg_len, lanes]
            cnt = jnp.minimum(pos0 + row + 1, w).astype(F32)
            d_ref[s * seg_len:(s + 1) * seg_len, lanes] = (acc * (1.0 / cnt) - cur).astype(BF16)

    pool_out = d_model // POOL_GROUPS
    parts = []
    for g in range(POOL_GROUPS):
        dg = d_ref[:, g * POOL_GROUP_DIM:(g + 1) * POOL_GROUP_DIM]
        parts.append(jnp.dot(dg, wpool_ref[g], preferred_element_type=F32))
    p_branch = jnp.concatenate(parts, axis=-1) * pscale_ref[...]

    wup = wup_ref[...]
    a_branch = jnp.concatenate(
        [lax.dot_general(oT_ref[s], wup, (((0,), (0,)), ((), ())), preferred_element_type=F32)
         for s in range(n_seg)], axis=0)
    m = gates[:, :d_model] * a_branch + gates[:, d_model:] * p_branch
    r = alpha * x + jnp.dot(m.astype(BF16), wo_ref[...], preferred_element_type=F32)
    h = _layer_norm(r, g1_ref[...], b1_ref[...])

    hb = h.astype(BF16)
    f = None
    lo = 0
    for width in FF_CHUNKS:
        gate = jnp.dot(hb, wgate_ref[:, lo:lo + width], preferred_element_type=F32)
        up = jnp.dot(hb, wupf_ref[:, lo:lo + width], preferred_element_type=F32)
        act = (gate / (1.0 + jnp.exp(-gate)) * up).astype(BF16)
        part = jnp.dot(act, wdown_ref[lo:lo + width, :], preferred_element_type=F32)
        f = part if f is None else f + part
        lo += width
    y_ref[...] = _layer_norm(alpha * h + f, g2_ref[...], b2_ref[...])


def _finish(x2, oT, u2, prefix, wg, bg, wup, wpool, pscale, wo, g1, b1, wgate, wupf, wdown, g2, b2,
         *, seg_len, pos0_fn, alpha):
    n, d = x2.shape
    tm = TOKEN_TILE
    n_seg = tm // seg_len
    _, hv, t_o = oT.shape
    assert sum(FF_CHUNKS) == wgate.shape[1]
    if n_seg == 1:
        tiles_per_row = t_o // tm
        o_spec = pl.BlockSpec((1, hv, tm), lambda i: (i // tiles_per_row, 0, i % tiles_per_row))
    else:
        assert t_o == seg_len
        o_spec = pl.BlockSpec((n_seg, hv, seg_len), lambda i: (i, 0, 0))
    kern = functools.partial(_finish_kernel, seg_len=seg_len, n_seg=n_seg, pos0_fn=pos0_fn, alpha=alpha)
    return pl.pallas_call(
        kern,
        grid=(n // tm,),
        in_specs=[
            pl.BlockSpec((tm, d), lambda i: (i, 0)),
            o_spec,
            pl.BlockSpec((tm, POOL_WIDTH), lambda i: (i, 0)),
            pl.BlockSpec((n_seg, POOL_HALO, POOL_WIDTH), lambda i: (i, 0, 0)),
            _const_spec(wg.shape), _const_spec(bg.shape), _const_spec(wup.shape),
            _const_spec(wpool.shape), _const_spec(pscale.shape), _const_spec(wo.shape),
            _const_spec(g1.shape), _const_spec(b1.shape),
            _const_spec(wgate.shape), _const_spec(wupf.shape), _const_spec(wdown.shape),
            _const_spec(g2.shape), _const_spec(b2.shape),
        ],
        out_specs=pl.BlockSpec((tm, d), lambda i: (i, 0)),
        out_shape=jax.ShapeDtypeStruct((n, d), F32),
        scratch_shapes=[
            pltpu.VMEM((POOL_HALO + seg_len, POOL_WIDTH), F32),
            pltpu.VMEM((tm, POOL_WIDTH), BF16),
        ],
        compiler_params=_params("parallel"),
        name="finish",
    )(x2, oT, u2, prefix, wg, bg, wup, wpool, pscale, wo, g1, b1, wgate, wupf, wdown, g2, b2)


def _prompt_pos0(tiles_per_seq, i):
    return (i % tiles_per_seq) * TOKEN_TILE


def _sample_pos0(past, i):
    return past


def kernel(x_prompt, x_sample, cache_kv_latent, cache_k_rope, state_pool, w_in, b_gate, q_norm_g, w_uq,
           kv_norm_g, w_uk, w_uv, w_attn_up, w_pool, pool_scale, w_o, ln1_g, ln1_b, w_gate, w_up, w_down,
           ln2_g, ln2_b):
    depth = w_in.shape[0]
    bp, n_p, d = x_prompt.shape
    bs, n_s, _ = x_sample.shape
    past = cache_kv_latent.shape[2]
    alpha = (2 * depth) ** 0.25
    half = QK_ROPE // 2
    o1 = Q_LORA
    o2 = o1 + KV_LORA
    o3 = o2 + QK_ROPE
    o4 = o3 + POOL_WIDTH

    pos_p = jnp.arange(n_p, dtype=jnp.int32)
    pos_s = past + (jnp.arange(bs * n_s, dtype=jnp.int32) % n_s)

    h_p, h_s = x_prompt, x_sample
    c_p_list, kr_p_list, pool_p_list = [], [], []
    c_s_list, kr_s_list, pool_s_list = [], [], []
    for l in range(depth):
        wl = w_in[l]
        w_kr = wl[:, o2:o3]
        w_kr_rot = jnp.concatenate([-w_kr[:, half:], w_kr[:, :half]], axis=1)
        w1 = jnp.concatenate([wl[:, :o2], wl[:, o3:o4], w_kr, w_kr_rot], axis=1).astype(BF16)
        wg = wl[:, o4:].astype(BF16)
        bg = b_gate[l][None, :]
        wq = _fold_q(w_uq[l], w_uk[l])
        qg = q_norm_g[l][None, :]
        kvg = kv_norm_g[l][None, :]
        wuvT = jnp.transpose(w_uv[l], (1, 2, 0)).astype(BF16)
        wuv = wuvT.reshape(N_HEADS * V_HEAD, KV_LORA)
        wup = w_attn_up[l].astype(BF16)
        wpool = w_pool[l].astype(BF16)
        pscale = pool_scale[l][None, :]
        wo = w_o[l].astype(BF16)
        g1, b1 = ln1_g[l][None, :], ln1_b[l][None, :]
        g2, b2 = ln2_g[l][None, :], ln2_b[l][None, :]
        wgate, wupf, wdown = w_gate[l].astype(BF16), w_up[l].astype(BF16), w_down[l].astype(BF16)

        qT, kcat, vT, c, kr, u, qn, ksq = _mixer_in(h_p, pos_p, w1, wq, wuv, qg, kvg)
        kmax = jnp.sqrt(jnp.max(ksq[:, :, 0, 0], axis=1))
        oT = _attention(qT, qn, kmax, kcat, vT, tq=Q_TILE, tk=KV_TILE, q_offset=0, n_keys=n_p)
        tiles_per_seq = n_p // TOKEN_TILE
        tails = u.reshape(bp, tiles_per_seq, TOKEN_TILE, POOL_WIDTH)[:, :, TOKEN_TILE - POOL_HALO:, :]
        prefix = jnp.concatenate([jnp.zeros_like(tails[:, :1]), tails[:, :-1]], axis=1)
        prefix = prefix.reshape(bp * tiles_per_seq, POOL_HALO, POOL_WIDTH)
        y = _finish(h_p.reshape(bp * n_p, d), oT, u.reshape(bp * n_p, POOL_WIDTH), prefix,
                 wg, bg, wup, wpool, pscale, wo, g1, b1, wgate, wupf, wdown, g2, b2, seg_len=TOKEN_TILE,
                 pos0_fn=functools.partial(_prompt_pos0, tiles_per_seq), alpha=alpha)
        c_p_list.append(c)
        kr_p_list.append(kr)
        pool_p_list.append(u[:, n_p - POOL_STATE:])
        h_p = y.reshape(bp, n_p, d)

        xs = h_s.reshape(1, bs * n_s, d)
        qT, _, _, c, kr, u, _, _ = _mixer_in(xs, pos_s, w1, wq, wuv, qg, kvg, stream_len=n_s)
        c = c.reshape(bs, n_s, KV_LORA)
        kr = kr.reshape(bs, n_s, QK_ROPE)
        u = u.reshape(bs, n_s, POOL_WIDTH)
        oT = _sample_attention(qT, cache_kv_latent[l], cache_k_rope[l], c, kr, wuvT)
        prefix = jnp.pad(state_pool[l], ((0, 0), (POOL_HALO - POOL_STATE, 0), (0, 0)))
        y = _finish(h_s.reshape(bs * n_s, d), oT, u.reshape(bs * n_s, POOL_WIDTH), prefix,
                 wg, bg, wup, wpool, pscale, wo, g1, b1, wgate, wupf, wdown, g2, b2, seg_len=n_s,
                 pos0_fn=functools.partial(_sample_pos0, past), alpha=alpha)
        u_all = jnp.concatenate([state_pool[l], u], axis=1)
        c_s_list.append(c)
        kr_s_list.append(kr)
        pool_s_list.append(u_all[:, u_all.shape[1] - POOL_STATE:])
        h_s = y.reshape(bs, n_s, d)

    def stack(parts):
        return parts[0][None] if len(parts) == 1 else jnp.stack(parts)

    return (h_p, h_s, stack(c_p_list), stack(kr_p_list), stack(pool_p_list),
            stack(c_s_list), stack(kr_s_list), stack(pool_s_list))
```

```python
import functools
import math

import jax
import jax.numpy as jnp
from jax import lax
from jax.experimental import pallas as pl
from jax.experimental.pallas import tpu as pltpu

F32 = jnp.float32
BF16 = jnp.bfloat16

LANES = 128
SUBLANES = 8
BF16_SUBLANES = 16

N_HEADS = 8
QK_NOPE = 64
QK_ROPE = 32
QK_HEAD = QK_NOPE + QK_ROPE
V_HEAD = 64
Q_LORA = 256
KV_LORA = 128
QK_CAT = KV_LORA + QK_ROPE
Q_ROWS = KV_LORA + 2 * QK_ROPE
VH_ROWS = V_HEAD + BF16_SUBLANES
CHUNK = 64
CHUNK_SHIFT = CHUNK.bit_length() - 1
ROPE_THETA = 10000.0
ATTN_SCALE = QK_HEAD ** -0.5
POOL_WIDTH = 512
POOL_GROUPS = 4
POOL_GROUP_DIM = POOL_WIDTH // POOL_GROUPS
POOL_WINDOWS = (2, 4, 8, 16)
POOL_STATE = max(POOL_WINDOWS) - 1
POOL_HALO = POOL_STATE + 1
RMS_EPS = 1e-6
LN_EPS = 1e-5
NEG_INF = -1e30
L_MIN = 2.0 ** -60
Q_SCALE = ATTN_SCALE * math.log2(math.e)

VMEM_LIMIT_BYTES = 56 * 1024 * 1024

TOKEN_TILE = 512
MIXER_TILE = 1024
KV_TILE = 512
Q_TILE = 512
KV_TILES_PER_TRIP = 4
FF_CHUNKS = (768, 768, 768, 512)


def _const_spec(shape):
    nd = len(shape)
    return pl.BlockSpec(shape, lambda *_: (0,) * nd, pipeline_mode=pl.Buffered(1))


def _params(*sem):
    return pltpu.CompilerParams(dimension_semantics=sem, vmem_limit_bytes=VMEM_LIMIT_BYTES)


def _fold_q_kernel(wuk_ref, wuqn_ref, wuqr_ref, out_ref):
    half = QK_ROPE // 2
    for h in range(N_HEADS):
        lat = jnp.dot(wuk_ref[h], wuqn_ref[h], preferred_element_type=F32,
                      precision=lax.Precision.HIGHEST)
        rope = wuqr_ref[h]
        out_ref[h, 0:KV_LORA, :] = lat.astype(BF16)
        out_ref[h, KV_LORA:KV_LORA + QK_ROPE, :] = rope.astype(BF16)
        out_ref[h, KV_LORA + QK_ROPE:KV_LORA + QK_ROPE + half, :] = (-rope[half:]).astype(BF16)
        out_ref[h, KV_LORA + QK_ROPE + half:Q_ROWS, :] = rope[:half].astype(BF16)


def _fold_q(w_uq, w_uk):
    wq = w_uq.reshape(Q_LORA, N_HEADS, QK_HEAD)
    wuqn_t = jnp.transpose(wq[:, :, :QK_NOPE], (1, 2, 0))
    wuqr_t = jnp.transpose(wq[:, :, QK_NOPE:], (1, 2, 0))
    wuk_t = jnp.transpose(w_uk, (1, 0, 2))
    out = pl.pallas_call(
        _fold_q_kernel,
        out_shape=jax.ShapeDtypeStruct((N_HEADS, Q_ROWS, Q_LORA), BF16),
        name="fold_q",
    )(wuk_t, wuqn_t, wuqr_t)
    return out.reshape(N_HEADS * Q_ROWS, Q_LORA)


def _mixer_in_kernel(x_ref, w1_ref, wq_ref, wuv_ref, qg_ref, kvg_ref, cosr_ref, sinr_ref, cost_ref, sint_ref,
                     qT_ref, kcat_ref, vT_ref, ckv_ref, kr_ref, u_ref, qn_ref, ksq_ref, *, stream_len):
    xb = x_ref[...].astype(BF16)
    z = jnp.dot(xb, w1_ref[...], preferred_element_type=F32)
    o1 = Q_LORA
    o2 = o1 + KV_LORA
    o3 = o2 + POOL_WIDTH
    o4 = o3 + QK_ROPE
    zq, zkv, u = z[:, :o1], z[:, o1:o2], z[:, o2:o3]
    zkr, zkr_rot = z[:, o3:o4], z[:, o4:o4 + QK_ROPE]
    u_ref[0] = u

    c = zkv * lax.rsqrt(jnp.mean(zkv * zkv, axis=-1, keepdims=True) + RMS_EPS) * kvg_ref[...]
    ckv_ref[0] = c
    kr = zkr * cosr_ref[...] + zkr_rot * sinr_ref[...]
    kr_ref[0] = kr
    cb = c.astype(BF16)
    krb = kr.astype(BF16)
    kcat_ref[0, :, 0:KV_LORA] = cb
    kcat_ref[0, :, KV_LORA:QK_CAT] = krb
    cf = cb.astype(F32)
    krf = krb.astype(F32)
    k_sq = jnp.sum(cf * cf, axis=-1, keepdims=True) + jnp.sum(krf * krf, axis=-1, keepdims=True)
    ksq_ref[0, 0] = jnp.broadcast_to(jnp.max(k_sq, axis=0, keepdims=True), ksq_ref.shape[2:])
    v_t = jnp.dot(wuv_ref[...], c.T.astype(BF16), preferred_element_type=F32)
    kv_tile = vT_ref.shape[-1]
    ones_row = lax.broadcasted_iota(jnp.int32, (VH_ROWS - V_HEAD, kv_tile), 0) == 0
    ones_rows = jnp.where(ones_row, 1.0, 0.0).astype(BF16)
    for t in range(c.shape[0] // kv_tile):
        for h in range(N_HEADS):
            vT_ref[0, t, h, 0:V_HEAD, :] = v_t[h * V_HEAD:(h + 1) * V_HEAD,
                                               t * kv_tile:(t + 1) * kv_tile].astype(BF16)
            vT_ref[0, t, h, V_HEAD:VH_ROWS, :] = ones_rows

    zqn = zq * lax.rsqrt(jnp.mean(zq * zq, axis=-1, keepdims=True) + RMS_EPS) * (qg_ref[...] * Q_SCALE)
    zqn_t = zqn.T.astype(BF16)
    q_all = jnp.dot(wq_ref[...], zqn_t, preferred_element_type=F32)
    cos_t = cost_ref[...]
    sin_t = sint_ref[...]
    for h in range(N_HEADS):
        b = h * Q_ROWS
        lat = q_all[b:b + KV_LORA].astype(BF16)
        rope = (q_all[b + KV_LORA:b + KV_LORA + QK_ROPE] * cos_t
                + q_all[b + KV_LORA + QK_ROPE:b + Q_ROWS] * sin_t).astype(BF16)
        if stream_len is None:
            qT_ref[0, h, 0:KV_LORA, :] = lat
            qT_ref[0, h, KV_LORA:QK_CAT, :] = rope
        else:
            for s in range(lat.shape[1] // stream_len):
                src = slice(s * stream_len, (s + 1) * stream_len)
                dst = slice(h * stream_len, (h + 1) * stream_len)
                qT_ref[s, 0:KV_LORA, dst] = lat[:, src]
                qT_ref[s, KV_LORA:QK_CAT, dst] = rope[:, src]
        latf = lat.astype(F32)
        ropef = rope.astype(F32)
        q_sq = jnp.sum(latf * latf, axis=0, keepdims=True) + jnp.sum(ropef * ropef, axis=0, keepdims=True)
        qn_ref[0, h:h + 1, :] = jnp.sqrt(q_sq)


def _mixer_in(x, pos, w1, wq, wuv, qg, kvg, stream_len=None):
    bsz, t, d = x.shape
    tm = MIXER_TILE
    nt = t // tm
    kv_per_tile = tm // KV_TILE
    if stream_len is None:
        q_spec = pl.BlockSpec((1, N_HEADS, QK_CAT, tm), lambda b, i: (b, 0, 0, i))
        q_shape = jax.ShapeDtypeStruct((bsz, N_HEADS, QK_CAT, t), BF16)
    else:
        per_tile = tm // stream_len
        q_spec = pl.BlockSpec((per_tile, QK_CAT, N_HEADS * stream_len), lambda b, i: (b * nt + i, 0, 0))
        q_shape = jax.ShapeDtypeStruct((bsz * t // stream_len, QK_CAT, N_HEADS * stream_len), BF16)
    half = QK_ROPE // 2
    inv = 1.0 / (ROPE_THETA ** (jnp.arange(half, dtype=F32) / half))
    ang = pos.astype(F32)[:, None] * inv[None, :]
    cos_r = jnp.tile(jnp.cos(ang), (1, 2))
    sin_r = jnp.tile(jnp.sin(ang), (1, 2))
    cos_t, sin_t = cos_r.T, sin_r.T
    n1 = w1.shape[1]
    outs = pl.pallas_call(
        functools.partial(_mixer_in_kernel, stream_len=stream_len),
        grid=(bsz, nt),
        in_specs=[
            pl.BlockSpec((None, tm, d), lambda b, i: (b, i, 0)),
            _const_spec((d, n1)),
            _const_spec((N_HEADS * Q_ROWS, Q_LORA)),
            _const_spec((N_HEADS * V_HEAD, KV_LORA)),
            _const_spec((1, Q_LORA)),
            _const_spec((1, KV_LORA)),
            pl.BlockSpec((tm, QK_ROPE), lambda b, i: (i, 0)),
            pl.BlockSpec((tm, QK_ROPE), lambda b, i: (i, 0)),
            pl.BlockSpec((QK_ROPE, tm), lambda b, i: (0, i)),
            pl.BlockSpec((QK_ROPE, tm), lambda b, i: (0, i)),
        ],
        out_specs=[
            q_spec,
            pl.BlockSpec((1, tm, QK_CAT), lambda b, i: (b, i, 0)),
            pl.BlockSpec((1, kv_per_tile, N_HEADS, VH_ROWS, KV_TILE), lambda b, i: (b, i, 0, 0, 0)),
            pl.BlockSpec((1, tm, KV_LORA), lambda b, i: (b, i, 0)),
            pl.BlockSpec((1, tm, QK_ROPE), lambda b, i: (b, i, 0)),
            pl.BlockSpec((1, tm, POOL_WIDTH), lambda b, i: (b, i, 0)),
            pl.BlockSpec((1, N_HEADS, tm), lambda b, i: (b, 0, i)),
            pl.BlockSpec((1, 1, SUBLANES, LANES), lambda b, i: (b, i, 0, 0)),
        ],
        out_shape=[
            q_shape,
            jax.ShapeDtypeStruct((bsz, t, QK_CAT), BF16),
            jax.ShapeDtypeStruct((bsz, t // KV_TILE, N_HEADS, VH_ROWS, KV_TILE), BF16),
            jax.ShapeDtypeStruct((bsz, t, KV_LORA), F32),
            jax.ShapeDtypeStruct((bsz, t, QK_ROPE), F32),
            jax.ShapeDtypeStruct((bsz, t, POOL_WIDTH), F32),
            jax.ShapeDtypeStruct((bsz, N_HEADS, t), F32),
            jax.ShapeDtypeStruct((bsz, nt, SUBLANES, LANES), F32),
        ],
        compiler_params=_params("parallel", "parallel"),
        name="mixer_in",
    )(x, w1, wq, wuv, qg, kvg, cos_r, sin_r, cos_t, sin_t)
    return outs


def _attention_kernel(kmax_ref, qT_ref, qn_ref, kcat_ref, vT_ref, oT_ref, m_ref, alpha_ref, acc_ref,
                      s_ref, mt_ref, p_ref, *, tq, tk, q_offset, n_keys, n_tiles):
    den = V_HEAD

    def values(j, h):
        return vT_ref[0, j, h]

    i = pl.program_id(1)
    q0 = q_offset + i * tq
    first_end = (q0 // CHUNK + 1) * CHUNK
    last_end = ((q0 + tq - 1) // CHUNK + 1) * CHUNK
    n_all = (jnp.minimum(last_end, n_keys) + tk - 1) // tk
    n_full = jnp.minimum(first_end // tk, n_all)
    last = N_HEADS - 1

    def scores(j, h):
        k = kcat_ref[0, pl.ds(pl.multiple_of(j * tk, tk), tk), :]
        return jnp.dot(k, qT_ref[0, h], preferred_element_type=F32)

    def visible(j):
        kchunk = lax.shift_right_arithmetic(j * tk + lax.broadcasted_iota(jnp.int32, (tk, 1), 0), CHUNK_SHIFT)
        qchunk = lax.shift_right_arithmetic(q0 + lax.broadcasted_iota(jnp.int32, (1, tq), 1), CHUNK_SHIFT)
        return kchunk <= qchunk

    def run_tiles(first_stage, tile_body, final_stage, tiles_per_trip=1):
        def loop(lo, hi, masked, per_trip=1):
            def body(t, carry):
                for r in range(per_trip):
                    tile_body(per_trip * t + r, masked)
                return carry

            lax.fori_loop(lo, hi, body, 0)

        first_stage(0, False)
        n_grouped = (n_full // tiles_per_trip) * tiles_per_trip
        loop(0, n_full // tiles_per_trip, False, per_trip=tiles_per_trip)
        loop(n_grouped, n_full, False)
        first_stage(jnp.minimum(n_full, n_tiles - 1), True)
        loop(n_full, n_all, True)
        if final_stage is not None:
            final_stage(n_all - 1)

    def stage_p(j, h, slot, masked):
        x = scores(j, h) - qn_ref[0, h:h + 1, :] * kmax_ref[pl.program_id(0)]
        if masked:
            x = jnp.where(visible(j), x, NEG_INF)
        p_ref[slot] = jnp.exp2(x).astype(BF16)

    def stage_v(j, h, slot):
        acc_ref[h] += jnp.dot(values(j, h), p_ref[slot], preferred_element_type=F32)

    def bounded_tile(j, masked):
        j_next = jnp.minimum(j + 1, n_tiles - 1)
        for h in range(N_HEADS):
            if h < last:
                stage_p(j, h + 1, (h + 1) % 2, masked)
            else:
                stage_p(j_next, 0, 0, masked)
            stage_v(j, h, h % 2)

    def bounded_path():
        acc_ref[...] = jnp.zeros(acc_ref.shape, F32)
        run_tiles(lambda j, masked: stage_p(j, 0, 0, masked), bounded_tile, None,
                  tiles_per_trip=KV_TILES_PER_TRIP)

    def stage_a(j, h, slot):
        s = scores(j, h)
        s_ref[slot] = s
        mt_ref[slot] = jnp.max(s, axis=0, keepdims=True)

    def stage_b(j, h, slot, masked):
        s = s_ref[slot]
        mt = mt_ref[slot]
        if masked:
            s = jnp.where(visible(j), s, NEG_INF)
            mt = jnp.max(s, axis=0, keepdims=True)
        m_old = m_ref[h]
        m_new = jnp.maximum(m_old, mt)
        m_ref[h] = m_new
        alpha_ref[h] = jnp.exp2(m_old - m_new)
        p_ref[slot] = jnp.exp2(s - m_new).astype(BF16)

    def stage_c(j, h, slot):
        pv = jnp.dot(values(j, h), p_ref[slot], preferred_element_type=F32)
        acc_ref[h] = alpha_ref[h] * acc_ref[h] + pv

    def exact_tile(j, masked):
        j_next = jnp.minimum(j + 1, n_tiles - 1)
        j_prev = jnp.maximum(j - 1, 0)
        for h in range(N_HEADS):
            if h < last:
                stage_a(j, h + 1, (h + 1) % 2)
            else:
                stage_a(j_next, 0, 0)
            stage_b(j, h, h % 2, masked)
            if h > 0:
                stage_c(j, h - 1, (h - 1) % 2)
            else:
                stage_c(j_prev, last, last % 2)

    def exact_path():
        m_ref[...] = jnp.full(m_ref.shape, NEG_INF, F32)
        acc_ref[...] = jnp.zeros(acc_ref.shape, F32)
        p_ref[last % 2] = jnp.zeros((tk, tq), BF16)
        alpha_ref[last] = jnp.ones((1, tq), F32)
        run_tiles(lambda j, masked: stage_a(j, 0, 0), exact_tile,
                  lambda j: stage_c(j, last, last % 2))

    bounded_path()
    denom = jnp.concatenate([acc_ref[h, den:den + 1, :] for h in range(N_HEADS)], axis=0)
    pl.when(jnp.logical_not(jnp.min(denom) >= L_MIN))(exact_path)

    for h in range(N_HEADS):
        o_t = acc_ref[h, 0:den, :] / acc_ref[h, den:den + 1, :]
        oT_ref[0, h * V_HEAD:(h + 1) * V_HEAD, :] = o_t.astype(BF16)


def _attention(qT, qn, kmax, kcat, vT, *, tq, tk, q_offset, n_keys):
    bsz, _, _, t_q = qT.shape
    t_k = kcat.shape[1]
    n_tiles = t_k // tk
    kern = functools.partial(_attention_kernel, tq=tq, tk=tk, q_offset=q_offset, n_keys=n_keys,
                             n_tiles=n_tiles)
    resident = pl.Buffered(1)
    return pl.pallas_call(
        kern,
        grid=(bsz, t_q // tq),
        in_specs=[
            pl.BlockSpec(memory_space=pltpu.SMEM),
            pl.BlockSpec((1, N_HEADS, QK_CAT, tq), lambda b, i: (b, 0, 0, i)),
            pl.BlockSpec((1, N_HEADS, tq), lambda b, i: (b, 0, i)),
            pl.BlockSpec((1, t_k, QK_CAT), lambda b, i: (b, 0, 0), pipeline_mode=resident),
            pl.BlockSpec((1, n_tiles, N_HEADS, VH_ROWS, tk), lambda b, i: (b, 0, 0, 0, 0),
                         pipeline_mode=resident),
        ],
        out_specs=pl.BlockSpec((1, N_HEADS * V_HEAD, tq), lambda b, i: (b, 0, i)),
        out_shape=jax.ShapeDtypeStruct((bsz, N_HEADS * V_HEAD, t_q), BF16),
        scratch_shapes=[
            pltpu.VMEM((N_HEADS, 1, tq), F32),
            pltpu.VMEM((N_HEADS, 1, tq), F32),
            pltpu.VMEM((N_HEADS, VH_ROWS, tq), F32),
            pltpu.VMEM((2, tk, tq), F32),
            pltpu.VMEM((2, 1, tq), F32),
            pltpu.VMEM((2, tk, tq), BF16),
        ],
        compiler_params=_params("parallel", "arbitrary"),
        name="attention",
    )(kmax, qT, qn, kcat, vT)


def _sample_attention_kernel(qT_ref, cc_ref, ckr_ref, cn_ref, krn_ref, wuvT_ref, oT_ref, *, n_s, past):
    lanes = N_HEADS * n_s
    q_t = qT_ref[0]
    cc = cc_ref[0].astype(BF16)
    cn = cn_ref[0].astype(BF16)
    k_cached = jnp.concatenate([cc, ckr_ref[0].astype(BF16)], axis=1)
    k_new = jnp.concatenate([cn, krn_ref[0].astype(BF16)], axis=1)
    s_c = jnp.dot(k_cached, q_t, preferred_element_type=F32)
    s_n = jnp.dot(k_new, q_t, preferred_element_type=F32)

    q_idx = lax.rem(lax.broadcasted_iota(jnp.int32, (1, lanes), 1), n_s)
    qchunk = lax.shift_right_arithmetic(past + q_idx, CHUNK_SHIFT)
    kchunk_c = lax.shift_right_arithmetic(lax.broadcasted_iota(jnp.int32, (past, 1), 0), CHUNK_SHIFT)
    kchunk_n = lax.shift_right_arithmetic(past + lax.broadcasted_iota(jnp.int32, (n_s, 1), 0), CHUNK_SHIFT)
    s_c = jnp.where(kchunk_c <= qchunk, s_c, NEG_INF)
    s_n = jnp.where(kchunk_n <= qchunk, s_n, NEG_INF)

    m = jnp.maximum(jnp.max(s_c, axis=0, keepdims=True), jnp.max(s_n, axis=0, keepdims=True))
    p_c = jnp.exp2(s_c - m)
    p_n = jnp.exp2(s_n - m)
    denom = jnp.sum(p_c, axis=0, keepdims=True) + jnp.sum(p_n, axis=0, keepdims=True)
    contract_rows = (((0,), (0,)), ((), ()))
    att_t = (lax.dot_general(cc, p_c.astype(BF16), contract_rows, preferred_element_type=F32)
             + lax.dot_general(cn, p_n.astype(BF16), contract_rows, preferred_element_type=F32)) / denom
    att_t = att_t.astype(BF16)
    for h in range(N_HEADS):
        o_t = jnp.dot(wuvT_ref[h], att_t[:, h * n_s:(h + 1) * n_s], preferred_element_type=F32)
        oT_ref[0, h * V_HEAD:(h + 1) * V_HEAD, :] = o_t.astype(BF16)


def _sample_attention(qT, cache_c, cache_kr, c_new, kr_new, wuvT):
    bsz, past, _ = cache_c.shape
    n_s = c_new.shape[1]
    kern = functools.partial(_sample_attention_kernel, n_s=n_s, past=past)
    return pl.pallas_call(
        kern,
        grid=(bsz,),
        in_specs=[
            pl.BlockSpec((1, QK_CAT, N_HEADS * n_s), lambda b: (b, 0, 0)),
            pl.BlockSpec((1, past, KV_LORA), lambda b: (b, 0, 0)),
            pl.BlockSpec((1, past, QK_ROPE), lambda b: (b, 0, 0)),
            pl.BlockSpec((1, n_s, KV_LORA), lambda b: (b, 0, 0)),
            pl.BlockSpec((1, n_s, QK_ROPE), lambda b: (b, 0, 0)),
            _const_spec((N_HEADS, V_HEAD, KV_LORA)),
        ],
        out_specs=pl.BlockSpec((1, N_HEADS * V_HEAD, n_s), lambda b: (b, 0, 0)),
        out_shape=jax.ShapeDtypeStruct((bsz, N_HEADS * V_HEAD, n_s), BF16),
        compiler_params=_params("parallel"),
        name="sample_attention",
    )(qT, cache_c, cache_kr, c_new, kr_new, wuvT)


def _layer_norm(r, g, b):
    mu = jnp.mean(r, axis=-1, keepdims=True)
    cen = r - mu
    var = jnp.mean(cen * cen, axis=-1, keepdims=True)
    return cen * lax.rsqrt(var + LN_EPS) * g + b


def _finish_kernel(x_ref, oT_ref, u_ref, pre_ref, wg_ref, bg_ref, wup_ref, wpool_ref, pscale_ref, wo_ref,
                g1_ref, b1_ref, wgate_ref, wupf_ref, wdown_ref, g2_ref, b2_ref, y_ref, ext_ref, d_ref,
                *, seg_len, n_seg, pos0_fn, alpha):
    d_model = x_ref.shape[-1]
    x = x_ref[...]
    xb = x.astype(BF16)
    zg = jnp.dot(xb, wg_ref[...], preferred_element_type=F32) + bg_ref[...]
    gates = 1.0 / (1.0 + jnp.exp(-zg))

    pos0 = pos0_fn(pl.program_id(0))
    row = lax.broadcasted_iota(jnp.int32, (seg_len, 1), 0)
    for s in range(n_seg):
        ext_ref[0:POOL_HALO, :] = pre_ref[s]
        ext_ref[POOL_HALO:POOL_HALO + seg_len, :] = u_ref[s * seg_len:(s + 1) * seg_len, :]
        for g, w in enumerate(POOL_WINDOWS):
            lanes = slice(g * POOL_GROUP_DIM, (g + 1) * POOL_GROUP_DIM)
            cur = ext_ref[POOL_HALO:POOL_HALO + seg_len, lanes]
            acc = cur
            for k in range(1, w):
                acc = acc + ext_ref[POOL_HALO - k:POOL_HALO - k + seg_len, lanes]
            cnt = jnp.minimum(pos0 + row + 1, w).astype(F32)
            d_ref[s * seg_len:(s + 1) * seg_len, lanes] = (acc * (1.0 / cnt) - cur).astype(BF16)

    pool_out = d_model // POOL_GROUPS
    parts = []
    for g in range(POOL_GROUPS):
        dg = d_ref[:, g * POOL_GROUP_DIM:(g + 1) * POOL_GROUP_DIM]
        parts.append(jnp.dot(dg, wpool_ref[g], preferred_element_type=F32))
    p_branch = jnp.concatenate(parts, axis=-1) * pscale_ref[...]

    wup = wup_ref[...]
    a_branch = jnp.concatenate(
        [lax.dot_general(oT_ref[s], wup, (((0,), (0,)), ((), ())), preferred_element_type=F32)
         for s in range(n_seg)], axis=0)
    m = gates[:, :d_model] * a_branch + gates[:, d_model:] * p_branch
    r = alpha * x + jnp.dot(m.astype(BF16), wo_ref[...], preferred_element_type=F32)
    h = _layer_norm(r, g1_ref[...], b1_ref[...])

    hb = h.astype(BF16)
    f = None
    lo = 0
    for width in FF_CHUNKS:
        gate = jnp.dot(hb, wgate_ref[:, lo:lo + width], preferred_element_type=F32)
        up = jnp.dot(hb, wupf_ref[:, lo:lo + width], preferred_element_type=F32)
        act = (gate / (1.0 + jnp.exp(-gate)) * up).astype(BF16)
        part = jnp.dot(act, wdown_ref[lo:lo + width, :], preferred_element_type=F32)
        f = part if f is None else f + part
        lo += width
    y_ref[...] = _layer_norm(alpha * h + f, g2_ref[...], b2_ref[...])


def _finish(x2, oT, u2, prefix, wg, bg, wup, wpool, pscale, wo, g1, b1, wgate, wupf, wdown, g2, b2,
         *, seg_len, pos0_fn, alpha):
    n, d = x2.shape
    tm = TOKEN_TILE
    n_seg = tm // seg_len
    _, hv, t_o = oT.shape
    assert sum(FF_CHUNKS) == wgate.shape[1]
    if n_seg == 1:
        tiles_per_row = t_o // tm
        o_spec = pl.BlockSpec((1, hv, tm), lambda i: (i // tiles_per_row, 0, i % tiles_per_row))
    else:
        assert t_o == seg_len
        o_spec = pl.BlockSpec((n_seg, hv, seg_len), lambda i: (i, 0, 0))
    kern = functools.partial(_finish_kernel, seg_len=seg_len, n_seg=n_seg, pos0_fn=pos0_fn, alpha=alpha)
    return pl.pallas_call(
        kern,
        grid=(n // tm,),
        in_specs=[
            pl.BlockSpec((tm, d), lambda i: (i, 0)),
            o_spec,
            pl.BlockSpec((tm, POOL_WIDTH), lambda i: (i, 0)),
            pl.BlockSpec((n_seg, POOL_HALO, POOL_WIDTH), lambda i: (i, 0, 0)),
            _const_spec(wg.shape), _const_spec(bg.shape), _const_spec(wup.shape),
            _const_spec(wpool.shape), _const_spec(pscale.shape), _const_spec(wo.shape),
            _const_spec(g1.shape), _const_spec(b1.shape),
            _const_spec(wgate.shape), _const_spec(wupf.shape), _const_spec(wdown.shape),
            _const_spec(g2.shape), _const_spec(b2.shape),
        ],
        out_specs=pl.BlockSpec((tm, d), lambda i: (i, 0)),
        out_shape=jax.ShapeDtypeStruct((n, d), F32),
        scratch_shapes=[
            pltpu.VMEM((POOL_HALO + seg_len, POOL_WIDTH), F32),
            pltpu.VMEM((tm, POOL_WIDTH), BF16),
        ],
        compiler_params=_params("parallel"),
        name="finish",
    )(x2, oT, u2, prefix, wg, bg, wup, wpool, pscale, wo, g1, b1, wgate, wupf, wdown, g2, b2)


def _prompt_pos0(tiles_per_seq, i):
    return (i % tiles_per_seq) * TOKEN_TILE


def _sample_pos0(past, i):
    return past


def kernel(x_prompt, x_sample, cache_kv_latent, cache_k_rope, state_pool, w_in, b_gate, q_norm_g, w_uq,
           kv_norm_g, w_uk, w_uv, w_attn_up, w_pool, pool_scale, w_o, ln1_g, ln1_b, w_gate, w_up, w_down,
           ln2_g, ln2_b):
    depth = w_in.shape[0]
    bp, n_p, d = x_prompt.shape
    bs, n_s, _ = x_sample.shape
    past = cache_kv_latent.shape[2]
    alpha = (2 * depth) ** 0.25
    half = QK_ROPE // 2
    o1 = Q_LORA
    o2 = o1 + KV_LORA
    o3 = o2 + QK_ROPE
    o4 = o3 + POOL_WIDTH

    pos_p = jnp.arange(n_p, dtype=jnp.int32)
    pos_s = past + (jnp.arange(bs * n_s, dtype=jnp.int32) % n_s)

    h_p, h_s = x_prompt, x_sample
    c_p_list, kr_p_list, pool_p_list = [], [], []
    c_s_list, kr_s_list, pool_s_list = [], [], []
    for l in range(depth):
        wl = w_in[l]
        w_kr = wl[:, o2:o3]
        w_kr_rot = jnp.concatenate([-w_kr[:, half:], w_kr[:, :half]], axis=1)
        w1 = jnp.concatenate([wl[:, :o2], wl[:, o3:o4], w_kr, w_kr_rot], axis=1).astype(BF16)
        wg = wl[:, o4:].astype(BF16)
        bg = b_gate[l][None, :]
        wq = _fold_q(w_uq[l], w_uk[l])
        qg = q_norm_g[l][None, :]
        kvg = kv_norm_g[l][None, :]
        wuvT = jnp.transpose(w_uv[l], (1, 2, 0)).astype(BF16)
        wuv = wuvT.reshape(N_HEADS * V_HEAD, KV_LORA)
        wup = w_attn_up[l].astype(BF16)
        wpool = w_pool[l].astype(BF16)
        pscale = pool_scale[l][None, :]
        wo = w_o[l].astype(BF16)
        g1, b1 = ln1_g[l][None, :], ln1_b[l][None, :]
        g2, b2 = ln2_g[l][None, :], ln2_b[l][None, :]
        wgate, wupf, wdown = w_gate[l].astype(BF16), w_up[l].astype(BF16), w_down[l].astype(BF16)

        qT, kcat, vT, c, kr, u, qn, ksq = _mixer_in(h_p, pos_p, w1, wq, wuv, qg, kvg)
        kmax = jnp.sqrt(jnp.max(ksq[:, :, 0, 0], axis=1))
        oT = _attention(qT, qn, kmax, kcat, vT, tq=Q_TILE, tk=KV_TILE, q_offset=0, n_keys=n_p)
        tiles_per_seq = n_p // TOKEN_TILE
        tails = u.reshape(bp, tiles_per_seq, TOKEN_TILE, POOL_WIDTH)[:, :, TOKEN_TILE - POOL_HALO:, :]
        prefix = jnp.concatenate([jnp.zeros_like(tails[:, :1]), tails[:, :-1]], axis=1)
        prefix = prefix.reshape(bp * tiles_per_seq, POOL_HALO, POOL_WIDTH)
        y = _finish(h_p.reshape(bp * n_p, d), oT, u.reshape(bp * n_p, POOL_WIDTH), prefix,
                 wg, bg, wup, wpool, pscale, wo, g1, b1, wgate, wupf, wdown, g2, b2, seg_len=TOKEN_TILE,
                 pos0_fn=functools.partial(_prompt_pos0, tiles_per_seq), alpha=alpha)
        c_p_list.append(c)
        kr_p_list.append(kr)
        pool_p_list.append(u[:, n_p - POOL_STATE:])
        h_p = y.reshape(bp, n_p, d)

        xs = h_s.reshape(1, bs * n_s, d)
        qT, _, _, c, kr, u, _, _ = _mixer_in(xs, pos_s, w1, wq, wuv, qg, kvg, stream_len=n_s)
        c = c.reshape(bs, n_s, KV_LORA)
        kr = kr.reshape(bs, n_s, QK_ROPE)
        u = u.reshape(bs, n_s, POOL_WIDTH)
        oT = _sample_attention(qT, cache_kv_latent[l], cache_k_rope[l], c, kr, wuvT)
        prefix = jnp.pad(state_pool[l], ((0, 0), (POOL_HALO - POOL_STATE, 0), (0, 0)))
        y = _finish(h_s.reshape(bs * n_s, d), oT, u.reshape(bs * n_s, POOL_WIDTH), prefix,
                 wg, bg, wup, wpool, pscale, wo, g1, b1, wgate, wupf, wdown, g2, b2, seg_len=n_s,
                 pos0_fn=functools.partial(_sample_pos0, past), alpha=alpha)
        u_all = jnp.concatenate([state_pool[l], u], axis=1)
        c_s_list.append(c)
        kr_s_list.append(kr)
        pool_s_list.append(u_all[:, u_all.shape[1] - POOL_STATE:])
        h_s = y.reshape(bs, n_s, d)

    def stack(parts):
        return parts[0][None] if len(parts) == 1 else jnp.stack(parts)

    return (h_p, h_s, stack(c_p_list), stack(kr_p_list), stack(pool_p_list),
            stack(c_s_list), stack(kr_s_list), stack(pool_s_list))
```

```python
import functools
import math

import jax
import jax.numpy as jnp
from jax import lax
from jax.experimental import pallas as pl
from jax.experimental.pallas import tpu as pltpu

F32 = jnp.float32
BF16 = jnp.bfloat16

LANES = 128
SUBLANES = 8
BF16_SUBLANES = 16

N_HEADS = 8
QK_NOPE = 64
QK_ROPE = 32
QK_HEAD = QK_NOPE + QK_ROPE
V_HEAD = 64
Q_LORA = 256
KV_LORA = 128
QK_CAT = KV_LORA + QK_ROPE
Q_ROWS = KV_LORA + 2 * QK_ROPE
VH_ROWS = V_HEAD + BF16_SUBLANES
CHUNK = 64
CHUNK_SHIFT = CHUNK.bit_length() - 1
ROPE_THETA = 10000.0
ATTN_SCALE = QK_HEAD ** -0.5
POOL_WIDTH = 512
POOL_GROUPS = 4
POOL_GROUP_DIM = POOL_WIDTH // POOL_GROUPS
POOL_WINDOWS = (2, 4, 8, 16)
POOL_STATE = max(POOL_WINDOWS) - 1
POOL_HALO = POOL_STATE + 1
RMS_EPS = 1e-6
LN_EPS = 1e-5
NEG_INF = -1e30
L_MIN = 2.0 ** -60
Q_SCALE = ATTN_SCALE * math.log2(math.e)

VMEM_LIMIT_BYTES = 56 * 1024 * 1024

TOKEN_TILE = 512
MIXER_TILE = 1024
KV_TILE = 512
Q_TILE = 512
KV_TILES_PER_TRIP = 4
FF_CHUNKS = (768, 768, 768, 512)


def _const_spec(shape):
    nd = len(shape)
    return pl.BlockSpec(shape, lambda *_: (0,) * nd, pipeline_mode=pl.Buffered(1))


def _params(*sem):
    return pltpu.CompilerParams(dimension_semantics=sem, vmem_limit_bytes=VMEM_LIMIT_BYTES)


def _fold_q_kernel(wuk_ref, wuqn_ref, wuqr_ref, out_ref):
    half = QK_ROPE // 2
    for h in range(N_HEADS):
        lat = jnp.dot(wuk_ref[h], wuqn_ref[h], preferred_element_type=F32,
                      precision=lax.Precision.HIGHEST)
        rope = wuqr_ref[h]
        out_ref[h, 0:KV_LORA, :] = lat.astype(BF16)
        out_ref[h, KV_LORA:KV_LORA + QK_ROPE, :] = rope.astype(BF16)
        out_ref[h, KV_LORA + QK_ROPE:KV_LORA + QK_ROPE + half, :] = (-rope[half:]).astype(BF16)
        out_ref[h, KV_LORA + QK_ROPE + half:Q_ROWS, :] = rope[:half].astype(BF16)


def _fold_q(w_uq, w_uk):
    wq = w_uq.reshape(Q_LORA, N_HEADS, QK_HEAD)
    wuqn_t = jnp.transpose(wq[:, :, :QK_NOPE], (1, 2, 0))
    wuqr_t = jnp.transpose(wq[:, :, QK_NOPE:], (1, 2, 0))
    wuk_t = jnp.transpose(w_uk, (1, 0, 2))
    out = pl.pallas_call(
        _fold_q_kernel,
        out_shape=jax.ShapeDtypeStruct((N_HEADS, Q_ROWS, Q_LORA), BF16),
        name="fold_q",
    )(wuk_t, wuqn_t, wuqr_t)
    return out.reshape(N_HEADS * Q_ROWS, Q_LORA)


def _mixer_in_kernel(x_ref, w1_ref, wq_ref, wuv_ref, qg_ref, kvg_ref, cosr_ref, sinr_ref, cost_ref, sint_ref,
                     qT_ref, kcat_ref, vT_ref, ckv_ref, kr_ref, u_ref, qn_ref, ksq_ref, *, stream_len):
    xb = x_ref[...].astype(BF16)
    z = jnp.dot(xb, w1_ref[...], preferred_element_type=F32)
    o1 = Q_LORA
    o2 = o1 + KV_LORA
    o3 = o2 + POOL_WIDTH
    o4 = o3 + QK_ROPE
    zq, zkv, u = z[:, :o1], z[:, o1:o2], z[:, o2:o3]
    zkr, zkr_rot = z[:, o3:o4], z[:, o4:o4 + QK_ROPE]
    u_ref[0] = u

    c = zkv * lax.rsqrt(jnp.mean(zkv * zkv, axis=-1, keepdims=True) + RMS_EPS) * kvg_ref[...]
    ckv_ref[0] = c
    kr = zkr * cosr_ref[...] + zkr_rot * sinr_ref[...]
    kr_ref[0] = kr
    cb = c.astype(BF16)
    krb = kr.astype(BF16)
    kcat_ref[0, :, 0:KV_LORA] = cb
    kcat_ref[0, :, KV_LORA:QK_CAT] = krb
    cf = cb.astype(F32)
    krf = krb.astype(F32)
    k_sq = jnp.sum(cf * cf, axis=-1, keepdims=True) + jnp.sum(krf * krf, axis=-1, keepdims=True)
    ksq_ref[0, 0] = jnp.broadcast_to(jnp.max(k_sq, axis=0, keepdims=True), ksq_ref.shape[2:])
    v_t = jnp.dot(wuv_ref[...], c.T.astype(BF16), preferred_element_type=F32)
    kv_tile = vT_ref.shape[-1]
    ones_row = lax.broadcasted_iota(jnp.int32, (VH_ROWS - V_HEAD, kv_tile), 0) == 0
    ones_rows = jnp.where(ones_row, 1.0, 0.0).astype(BF16)
    for t in range(c.shape[0] // kv_tile):
        for h in range(N_HEADS):
            vT_ref[0, t, h, 0:V_HEAD, :] = v_t[h * V_HEAD:(h + 1) * V_HEAD,
                                               t * kv_tile:(t + 1) * kv_tile].astype(BF16)
            vT_ref[0, t, h, V_HEAD:VH_ROWS, :] = ones_rows

    zqn = zq * lax.rsqrt(jnp.mean(zq * zq, axis=-1, keepdims=True) + RMS_EPS) * (qg_ref[...] * Q_SCALE)
    zqn_t = zqn.T.astype(BF16)
    q_all = jnp.dot(wq_ref[...], zqn_t, preferred_element_type=F32)
    cos_t = cost_ref[...]
    sin_t = sint_ref[...]
    for h in range(N_HEADS):
        b = h * Q_ROWS
        lat = q_all[b:b + KV_LORA].astype(BF16)
        rope = (q_all[b + KV_LORA:b + KV_LORA + QK_ROPE] * cos_t
                + q_all[b + KV_LORA + QK_ROPE:b + Q_ROWS] * sin_t).astype(BF16)
        if stream_len is None:
            qT_ref[0, h, 0:KV_LORA, :] = lat
            qT_ref[0, h, KV_LORA:QK_CAT, :] = rope
        else:
            for s in range(lat.shape[1] // stream_len):
                src = slice(s * stream_len, (s + 1) * stream_len)
                dst = slice(h * stream_len, (h + 1) * stream_len)
                qT_ref[s, 0:KV_LORA, dst] = lat[:, src]
                qT_ref[s, KV_LORA:QK_CAT, dst] = rope[:, src]
        latf = lat.astype(F32)
        ropef = rope.astype(F32)
        q_sq = jnp.sum(latf * latf, axis=0, keepdims=True) + jnp.sum(ropef * ropef, axis=0, keepdims=True)
        qn_ref[0, h:h + 1, :] = jnp.sqrt(q_sq)


def _mixer_in(x, pos, w1, wq, wuv, qg, kvg, stream_len=None):
    bsz, t, d = x.shape
    tm = MIXER_TILE
    nt = t // tm
    kv_per_tile = tm // KV_TILE
    if stream_len is None:
        q_spec = pl.BlockSpec((1, N_HEADS, QK_CAT, tm), lambda b, i: (b, 0, 0, i))
        q_shape = jax.ShapeDtypeStruct((bsz, N_HEADS, QK_CAT, t), BF16)
    else:
        per_tile = tm // stream_len
        q_spec = pl.BlockSpec((per_tile, QK_CAT, N_HEADS * stream_len), lambda b, i: (b * nt + i, 0, 0))
        q_shape = jax.ShapeDtypeStruct((bsz * t // stream_len, QK_CAT, N_HEADS * stream_len), BF16)
    half = QK_ROPE // 2
    inv = 1.0 / (ROPE_THETA ** (jnp.arange(half, dtype=F32) / half))
    ang = pos.astype(F32)[:, None] * inv[None, :]
    cos_r = jnp.tile(jnp.cos(ang), (1, 2))
    sin_r = jnp.tile(jnp.sin(ang), (1, 2))
    cos_t, sin_t = cos_r.T, sin_r.T
    n1 = w1.shape[1]
    outs = pl.pallas_call(
        functools.partial(_mixer_in_kernel, stream_len=stream_len),
        grid=(bsz, nt),
        in_specs=[
            pl.BlockSpec((None, tm, d), lambda b, i: (b, i, 0)),
            _const_spec((d, n1)),
            _const_spec((N_HEADS * Q_ROWS, Q_LORA)),
            _const_spec((N_HEADS * V_HEAD, KV_LORA)),
            _const_spec((1, Q_LORA)),
            _const_spec((1, KV_LORA)),
            pl.BlockSpec((tm, QK_ROPE), lambda b, i: (i, 0)),
            pl.BlockSpec((tm, QK_ROPE), lambda b, i: (i, 0)),
            pl.BlockSpec((QK_ROPE, tm), lambda b, i: (0, i)),
            pl.BlockSpec((QK_ROPE, tm), lambda b, i: (0, i)),
        ],
        out_specs=[
            q_spec,
            pl.BlockSpec((1, tm, QK_CAT), lambda b, i: (b, i, 0)),
            pl.BlockSpec((1, kv_per_tile, N_HEADS, VH_ROWS, KV_TILE), lambda b, i: (b, i, 0, 0, 0)),
            pl.BlockSpec((1, tm, KV_LORA), lambda b, i: (b, i, 0)),
            pl.BlockSpec((1, tm, QK_ROPE), lambda b, i: (b, i, 0)),
            pl.BlockSpec((1, tm, POOL_WIDTH), lambda b, i: (b, i, 0)),
            pl.BlockSpec((1, N_HEADS, tm), lambda b, i: (b, 0, i)),
            pl.BlockSpec((1, 1, SUBLANES, LANES), lambda b, i: (b, i, 0, 0)),
        ],
        out_shape=[
            q_shape,
            jax.ShapeDtypeStruct((bsz, t, QK_CAT), BF16),
            jax.ShapeDtypeStruct((bsz, t // KV_TILE, N_HEADS, VH_ROWS, KV_TILE), BF16),
            jax.ShapeDtypeStruct((bsz, t, KV_LORA), F32),
            jax.ShapeDtypeStruct((bsz, t, QK_ROPE), F32),
            jax.ShapeDtypeStruct((bsz, t, POOL_WIDTH), F32),
            jax.ShapeDtypeStruct((bsz, N_HEADS, t), F32),
            jax.ShapeDtypeStruct((bsz, nt, SUBLANES, LANES), F32),
        ],
        compiler_params=_params("parallel", "parallel"),
        name="mixer_in",
    )(x, w1, wq, wuv, qg, kvg, cos_r, sin_r, cos_t, sin_t)
    return outs


def _attention_kernel(kmax_ref, qT_ref, qn_ref, kcat_ref, vT_ref, oT_ref, m_ref, alpha_ref, acc_ref,
                      s_ref, mt_ref, p_ref, *, tq, tk, q_offset, n_keys, n_tiles):
    den = V_HEAD

    def values(j, h):
        return vT_ref[0, j, h]

    i = pl.program_id(1)
    q0 = q_offset + i * tq
    first_end = (q0 // CHUNK + 1) * CHUNK
    last_end = ((q0 + tq - 1) // CHUNK + 1) * CHUNK
    n_all = (jnp.minimum(last_end, n_keys) + tk - 1) // tk
    n_full = jnp.minimum(first_end // tk, n_all)
    last = N_HEADS - 1

    def scores(j, h):
        k = kcat_ref[0, pl.ds(pl.multiple_of(j * tk, tk), tk), :]
        return jnp.dot(k, qT_ref[0, h], preferred_element_type=F32)

    def visible(j):
        kchunk = lax.shift_right_arithmetic(j * tk + lax.broadcasted_iota(jnp.int32, (tk, 1), 0), CHUNK_SHIFT)
        qchunk = lax.shift_right_arithmetic(q0 + lax.broadcasted_iota(jnp.int32, (1, tq), 1), CHUNK_SHIFT)
        return kchunk <= qchunk

    def run_tiles(first_stage, tile_body, final_stage, tiles_per_trip=1):
        def loop(lo, hi, masked, per_trip=1):
            def body(t, carry):
                for r in range(per_trip):
                    tile_body(per_trip * t + r, masked)
                return carry

            lax.fori_loop(lo, hi, body, 0)

        first_stage(0)
        n_grouped = (n_full // tiles_per_trip) * tiles_per_trip
        loop(0, n_full // tiles_per_trip, False, per_trip=tiles_per_trip)
        loop(n_grouped, n_full, False)
        loop(n_full, n_all, True)
        if final_stage is not None:
            final_stage(n_all - 1)

    def stage_p(j, h, slot, masked):
        x = scores(j, h) - qn_ref[0, h:h + 1, :] * kmax_ref[pl.program_id(0)]
        if masked:
            x = jnp.where(visible(j), x, NEG_INF)
        p_ref[slot] = jnp.exp2(x).astype(BF16)

    def stage_v(j, h, slot):
        acc_ref[h] += jnp.dot(values(j, h), p_ref[slot], preferred_element_type=F32)

    def bounded_tile(j, masked):
        j_next = jnp.minimum(j + 1, n_tiles - 1)
        for h in range(N_HEADS):
            if h < last:
                stage_p(j, h + 1, (h + 1) % 2, masked)
            else:
                stage_p(j_next, 0, 0, True)
            stage_v(j, h, h % 2)

    def bounded_path():
        acc_ref[...] = jnp.zeros(acc_ref.shape, F32)
        run_tiles(lambda j: stage_p(j, 0, 0, True), bounded_tile, None,
                  tiles_per_trip=KV_TILES_PER_TRIP)

    def stage_a(j, h, slot):
        s = scores(j, h)
        s_ref[slot] = s
        mt_ref[slot] = jnp.max(s, axis=0, keepdims=True)

    def stage_b(j, h, slot, masked):
        s = s_ref[slot]
        mt = mt_ref[slot]
        if masked:
            s = jnp.where(visible(j), s, NEG_INF)
            mt = jnp.max(s, axis=0, keepdims=True)
        m_old = m_ref[h]
        m_new = jnp.maximum(m_old, mt)
        m_ref[h] = m_new
        alpha_ref[h] = jnp.exp2(m_old - m_new)
        p_ref[slot] = jnp.exp2(s - m_new).astype(BF16)

    def stage_c(j, h, slot):
        pv = jnp.dot(values(j, h), p_ref[slot], preferred_element_type=F32)
        acc_ref[h] = alpha_ref[h] * acc_ref[h] + pv

    def exact_tile(j, masked):
        j_next = jnp.minimum(j + 1, n_tiles - 1)
        j_prev = jnp.maximum(j - 1, 0)
        for h in range(N_HEADS):
            if h < last:
                stage_a(j, h + 1, (h + 1) % 2)
            else:
                stage_a(j_next, 0, 0)
            stage_b(j, h, h % 2, masked)
            if h > 0:
                stage_c(j, h - 1, (h - 1) % 2)
            else:
                stage_c(j_prev, last, last % 2)

    def exact_path():
        m_ref[...] = jnp.full(m_ref.shape, NEG_INF, F32)
        acc_ref[...] = jnp.zeros(acc_ref.shape, F32)
        p_ref[last % 2] = jnp.zeros((tk, tq), BF16)
        alpha_ref[last] = jnp.ones((1, tq), F32)
        run_tiles(lambda j: stage_a(j, 0, 0), exact_tile,
                  lambda j: stage_c(j, last, last % 2))

    bounded_path()
    denom = jnp.concatenate([acc_ref[h, den:den + 1, :] for h in range(N_HEADS)], axis=0)
    pl.when(jnp.logical_not(jnp.min(denom) >= L_MIN))(exact_path)

    for h in range(N_HEADS):
        o_t = acc_ref[h, 0:den, :] / acc_ref[h, den:den + 1, :]
        oT_ref[0, h * V_HEAD:(h + 1) * V_HEAD, :] = o_t.astype(BF16)


def _attention(qT, qn, kmax, kcat, vT, *, tq, tk, q_offset, n_keys):
    bsz, _, _, t_q = qT.shape
    t_k = kcat.shape[1]
    n_tiles = t_k // tk
    kern = functools.partial(_attention_kernel, tq=tq, tk=tk, q_offset=q_offset, n_keys=n_keys,
                             n_tiles=n_tiles)
    resident = pl.Buffered(1)
    return pl.pallas_call(
        kern,
        grid=(bsz, t_q // tq),
        in_specs=[
            pl.BlockSpec(memory_space=pltpu.SMEM),
            pl.BlockSpec((1, N_HEADS, QK_CAT, tq), lambda b, i: (b, 0, 0, i)),
            pl.BlockSpec((1, N_HEADS, tq), lambda b, i: (b, 0, i)),
            pl.BlockSpec((1, t_k, QK_CAT), lambda b, i: (b, 0, 0), pipeline_mode=resident),
            pl.BlockSpec((1, n_tiles, N_HEADS, VH_ROWS, tk), lambda b, i: (b, 0, 0, 0, 0),
                         pipeline_mode=resident),
        ],
        out_specs=pl.BlockSpec((1, N_HEADS * V_HEAD, tq), lambda b, i: (b, 0, i)),
        out_shape=jax.ShapeDtypeStruct((bsz, N_HEADS * V_HEAD, t_q), BF16),
        scratch_shapes=[
            pltpu.VMEM((N_HEADS, 1, tq), F32),
            pltpu.VMEM((N_HEADS, 1, tq), F32),
            pltpu.VMEM((N_HEADS, VH_ROWS, tq), F32),
            pltpu.VMEM((2, tk, tq), F32),
            pltpu.VMEM((2, 1, tq), F32),
            pltpu.VMEM((2, tk, tq), BF16),
        ],
        compiler_params=_params("parallel", "arbitrary"),
        name="attention",
    )(kmax, qT, qn, kcat, vT)


def _sample_attention_kernel(qT_ref, cc_ref, ckr_ref, cn_ref, krn_ref, wuvT_ref, oT_ref, *, n_s, past):
    lanes = N_HEADS * n_s
    q_t = qT_ref[0]
    cc = cc_ref[0].astype(BF16)
    cn = cn_ref[0].astype(BF16)
    k_cached = jnp.concatenate([cc, ckr_ref[0].astype(BF16)], axis=1)
    k_new = jnp.concatenate([cn, krn_ref[0].astype(BF16)], axis=1)
    s_c = jnp.dot(k_cached, q_t, preferred_element_type=F32)
    s_n = jnp.dot(k_new, q_t, preferred_element_type=F32)

    q_idx = lax.rem(lax.broadcasted_iota(jnp.int32, (1, lanes), 1), n_s)
    qchunk = lax.shift_right_arithmetic(past + q_idx, CHUNK_SHIFT)
    kchunk_c = lax.shift_right_arithmetic(lax.broadcasted_iota(jnp.int32, (past, 1), 0), CHUNK_SHIFT)
    kchunk_n = lax.shift_right_arithmetic(past + lax.broadcasted_iota(jnp.int32, (n_s, 1), 0), CHUNK_SHIFT)
    s_c = jnp.where(kchunk_c <= qchunk, s_c, NEG_INF)
    s_n = jnp.where(kchunk_n <= qchunk, s_n, NEG_INF)

    m = jnp.maximum(jnp.max(s_c, axis=0, keepdims=True), jnp.max(s_n, axis=0, keepdims=True))
    p_c = jnp.exp2(s_c - m)
    p_n = jnp.exp2(s_n - m)
    denom = jnp.sum(p_c, axis=0, keepdims=True) + jnp.sum(p_n, axis=0, keepdims=True)
    contract_rows = (((0,), (0,)), ((), ()))
    att_t = (lax.dot_general(cc, p_c.astype(BF16), contract_rows, preferred_element_type=F32)
             + lax.dot_general(cn, p_n.astype(BF16), contract_rows, preferred_element_type=F32)) / denom
    att_t = att_t.astype(BF16)
    for h in range(N_HEADS):
        o_t = jnp.dot(wuvT_ref[h], att_t[:, h * n_s:(h + 1) * n_s], preferred_element_type=F32)
        oT_ref[0, h * V_HEAD:(h + 1) * V_HEAD, :] = o_t.astype(BF16)


def _sample_attention(qT, cache_c, cache_kr, c_new, kr_new, wuvT):
    bsz, past, _ = cache_c.shape
    n_s = c_new.shape[1]
    kern = functools.partial(_sample_attention_kernel, n_s=n_s, past=past)
    return pl.pallas_call(
        kern,
        grid=(bsz,),
        in_specs=[
            pl.BlockSpec((1, QK_CAT, N_HEADS * n_s), lambda b: (b, 0, 0)),
            pl.BlockSpec((1, past, KV_LORA), lambda b: (b, 0, 0)),
            pl.BlockSpec((1, past, QK_ROPE), lambda b: (b, 0, 0)),
            pl.BlockSpec((1, n_s, KV_LORA), lambda b: (b, 0, 0)),
            pl.BlockSpec((1, n_s, QK_ROPE), lambda b: (b, 0, 0)),
            _const_spec((N_HEADS, V_HEAD, KV_LORA)),
        ],
        out_specs=pl.BlockSpec((1, N_HEADS * V_HEAD, n_s), lambda b: (b, 0, 0)),
        out_shape=jax.ShapeDtypeStruct((bsz, N_HEADS * V_HEAD, n_s), BF16),
        compiler_params=_params("parallel"),
        name="sample_attention",
    )(qT, cache_c, cache_kr, c_new, kr_new, wuvT)


def _layer_norm(r, g, b):
    mu = jnp.mean(r, axis=-1, keepdims=True)
    cen = r - mu
    var = jnp.mean(cen * cen, axis=-1, keepdims=True)
    return cen * lax.rsqrt(var + LN_EPS) * g + b


def _finish_kernel(x_ref, oT_ref, u_ref, pre_ref, wg_ref, bg_ref, wup_ref, wpool_ref, pscale_ref, wo_ref,
                g1_ref, b1_ref, wgate_ref, wupf_ref, wdown_ref, g2_ref, b2_ref, y_ref, ext_ref, d_ref,
                *, seg_len, n_seg, pos0_fn, alpha):
    d_model = x_ref.shape[-1]
    x = x_ref[...]
    xb = x.astype(BF16)
    zg = jnp.dot(xb, wg_ref[...], preferred_element_type=F32) + bg_ref[...]
    gates = 1.0 / (1.0 + jnp.exp(-zg))

    pos0 = pos0_fn(pl.program_id(0))
    row = lax.broadcasted_iota(jnp.int32, (seg_len, 1), 0)
    for s in range(n_seg):
        ext_ref[0:POOL_HALO, :] = pre_ref[s]
        ext_ref[POOL_HALO:POOL_HALO + seg_len, :] = u_ref[s * seg_len:(s + 1) * seg_len, :]
        for g, w in enumerate(POOL_WINDOWS):
            lanes = slice(g * POOL_GROUP_DIM, (g + 1) * POOL_GROUP_DIM)
            cur = ext_ref[POOL_HALO:POOL_HALO + seg_len, lanes]
            acc = cur
            for k in range(1, w):
                acc = acc + ext_ref[POOL_HALO - k:POOL_HALO - k + seg_len, lanes]
            cnt = jnp.minimum(pos0 + row + 1, w).astype(F32)
            d_ref[s * seg_len:(s + 1) * seg_len, lanes] = (acc * (1.0 / cnt) - cur).astype(BF16)

    pool_out = d_model // POOL_GROUPS
    parts = []
    for g in range(POOL_GROUPS):
        dg = d_ref[:, g * POOL_GROUP_DIM:(g + 1) * POOL_GROUP_DIM]
        parts.append(jnp.dot(dg, wpool_ref[g], preferred_element_type=F32))
    p_branch = jnp.concatenate(parts, axis=-1) * pscale_ref[...]

    wup = wup_ref[...]
    a_branch = jnp.concatenate(
        [lax.dot_general(oT_ref[s], wup, (((0,), (0,)), ((), ())), preferred_element_type=F32)
         for s in range(n_seg)], axis=0)
    m = gates[:, :d_model] * a_branch + gates[:, d_model:] * p_branch
    r = alpha * x + jnp.dot(m.astype(BF16), wo_ref[...], preferred_element_type=F32)
    h = _layer_norm(r, g1_ref[...], b1_ref[...])

    hb = h.astype(BF16)
    f = None
    lo = 0
    for width in FF_CHUNKS:
        gate = jnp.dot(hb, wgate_ref[:, lo:lo + width], preferred_element_type=F32)
        up = jnp.dot(hb, wupf_ref[:, lo:lo + width], preferred_element_type=F32)
        act = (gate / (1.0 + jnp.exp(-gate)) * up).astype(BF16)
        part = jnp.dot(act, wdown_ref[lo:lo + width, :], preferred_element_type=F32)
        f = part if f is None else f + part
        lo += width
    y_ref[...] = _layer_norm(alpha * h + f, g2_ref[...], b2_ref[...])


def _finish(x2, oT, u2, prefix, wg, bg, wup, wpool, pscale, wo, g1, b1, wgate, wupf, wdown, g2, b2,
         *, seg_len, pos0_fn, alpha):
    n, d = x2.shape
    tm = TOKEN_TILE
    n_seg = tm // seg_len
    _, hv, t_o = oT.shape
    assert sum(FF_CHUNKS) == wgate.shape[1]
    if n_seg == 1:
        tiles_per_row = t_o // tm
        o_spec = pl.BlockSpec((1, hv, tm), lambda i: (i // tiles_per_row, 0, i % tiles_per_row))
    else:
        assert t_o == seg_len
        o_spec = pl.BlockSpec((n_seg, hv, seg_len), lambda i: (i, 0, 0))
    kern = functools.partial(_finish_kernel, seg_len=seg_len, n_seg=n_seg, pos0_fn=pos0_fn, alpha=alpha)
    return pl.pallas_call(
        kern,
        grid=(n // tm,),
        in_specs=[
            pl.BlockSpec((tm, d), lambda i: (i, 0)),
            o_spec,
            pl.BlockSpec((tm, POOL_WIDTH), lambda i: (i, 0)),
            pl.BlockSpec((n_seg, POOL_HALO, POOL_WIDTH), lambda i: (i, 0, 0)),
            _const_spec(wg.shape), _const_spec(bg.shape), _const_spec(wup.shape),
            _const_spec(wpool.shape), _const_spec(pscale.shape), _const_spec(wo.shape),
            _const_spec(g1.shape), _const_spec(b1.shape),
            _const_spec(wgate.shape), _const_spec(wupf.shape), _const_spec(wdown.shape),
            _const_spec(g2.shape), _const_spec(b2.shape),
        ],
        out_specs=pl.BlockSpec((tm, d), lambda i: (i, 0)),
        out_shape=jax.ShapeDtypeStruct((n, d), F32),
        scratch_shapes=[
            pltpu.VMEM((POOL_HALO + seg_len, POOL_WIDTH), F32),
            pltpu.VMEM((tm, POOL_WIDTH), BF16),
        ],
        compiler_params=_params("parallel"),
        name="finish",
    )(x2, oT, u2, prefix, wg, bg, wup, wpool, pscale, wo, g1, b1, wgate, wupf, wdown, g2, b2)


def _prompt_pos0(tiles_per_seq, i):
    return (i % tiles_per_seq) * TOKEN_TILE


def _sample_pos0(past, i):
    return past


def kernel(x_prompt, x_sample, cache_kv_latent, cache_k_rope, state_pool, w_in, b_gate, q_norm_g, w_uq,
           kv_norm_g, w_uk, w_uv, w_attn_up, w_pool, pool_scale, w_o, ln1_g, ln1_b, w_gate, w_up, w_down,
           ln2_g, ln2_b):
    depth = w_in.shape[0]
    bp, n_p, d = x_prompt.shape
    bs, n_s, _ = x_sample.shape
    past = cache_kv_latent.shape[2]
    alpha = (2 * depth) ** 0.25
    half = QK_ROPE // 2
    o1 = Q_LORA
    o2 = o1 + KV_LORA
    o3 = o2 + QK_ROPE
    o4 = o3 + POOL_WIDTH

    pos_p = jnp.arange(n_p, dtype=jnp.int32)
    pos_s = past + (jnp.arange(bs * n_s, dtype=jnp.int32) % n_s)

    h_p, h_s = x_prompt, x_sample
    c_p_list, kr_p_list, pool_p_list = [], [], []
    c_s_list, kr_s_list, pool_s_list = [], [], []
    for l in range(depth):
        wl = w_in[l]
        w_kr = wl[:, o2:o3]
        w_kr_rot = jnp.concatenate([-w_kr[:, half:], w_kr[:, :half]], axis=1)
        w1 = jnp.concatenate([wl[:, :o2], wl[:, o3:o4], w_kr, w_kr_rot], axis=1).astype(BF16)
        wg = wl[:, o4:].astype(BF16)
        bg = b_gate[l][None, :]
        wq = _fold_q(w_uq[l], w_uk[l])
        qg = q_norm_g[l][None, :]
        kvg = kv_norm_g[l][None, :]
        wuvT = jnp.transpose(w_uv[l], (1, 2, 0)).astype(BF16)
        wuv = wuvT.reshape(N_HEADS * V_HEAD, KV_LORA)
        wup = w_attn_up[l].astype(BF16)
        wpool = w_pool[l].astype(BF16)
        pscale = pool_scale[l][None, :]
        wo = w_o[l].astype(BF16)
        g1, b1 = ln1_g[l][None, :], ln1_b[l][None, :]
        g2, b2 = ln2_g[l][None, :], ln2_b[l][None, :]
        wgate, wupf, wdown = w_gate[l].astype(BF16), w_up[l].astype(BF16), w_down[l].astype(BF16)

        qT, kcat, vT, c, kr, u, qn, ksq = _mixer_in(h_p, pos_p, w1, wq, wuv, qg, kvg)
        kmax = jnp.sqrt(jnp.max(ksq[:, :, 0, 0], axis=1))
        oT = _attention(qT, qn, kmax, kcat, vT, tq=Q_TILE, tk=KV_TILE, q_offset=0, n_keys=n_p)
        tiles_per_seq = n_p // TOKEN_TILE
        tails = u.reshape(bp, tiles_per_seq, TOKEN_TILE, POOL_WIDTH)[:, :, TOKEN_TILE - POOL_HALO:, :]
        prefix = jnp.concatenate([jnp.zeros_like(tails[:, :1]), tails[:, :-1]], axis=1)
        prefix = prefix.reshape(bp * tiles_per_seq, POOL_HALO, POOL_WIDTH)
        y = _finish(h_p.reshape(bp * n_p, d), oT, u.reshape(bp * n_p, POOL_WIDTH), prefix,
                 wg, bg, wup, wpool, pscale, wo, g1, b1, wgate, wupf, wdown, g2, b2, seg_len=TOKEN_TILE,
                 pos0_fn=functools.partial(_prompt_pos0, tiles_per_seq), alpha=alpha)
        c_p_list.append(c)
        kr_p_list.append(kr)
        pool_p_list.append(u[:, n_p - POOL_STATE:])
        h_p = y.reshape(bp, n_p, d)

        xs = h_s.reshape(1, bs * n_s, d)
        qT, _, _, c, kr, u, _, _ = _mixer_in(xs, pos_s, w1, wq, wuv, qg, kvg, stream_len=n_s)
        c = c.reshape(bs, n_s, KV_LORA)
        kr = kr.reshape(bs, n_s, QK_ROPE)
        u = u.reshape(bs, n_s, POOL_WIDTH)
        oT = _sample_attention(qT, cache_kv_latent[l], cache_k_rope[l], c, kr, wuvT)
        prefix = jnp.pad(state_pool[l], ((0, 0), (POOL_HALO - POOL_STATE, 0), (0, 0)))
        y = _finish(h_s.reshape(bs * n_s, d), oT, u.reshape(bs * n_s, POOL_WIDTH), prefix,
                 wg, bg, wup, wpool, pscale, wo, g1, b1, wgate, wupf, wdown, g2, b2, seg_len=n_s,
                 pos0_fn=functools.partial(_sample_pos0, past), alpha=alpha)
        u_all = jnp.concatenate([state_pool[l], u], axis=1)
        c_s_list.append(c)
        kr_s_list.append(kr)
        pool_s_list.append(u_all[:, u_all.shape[1] - POOL_STATE:])
        h_s = y.reshape(bs, n_s, d)

    def stack(parts):
        return parts[0][None] if len(parts) == 1 else jnp.stack(parts)

    return (h_p, h_s, stack(c_p_list), stack(kr_p_list), stack(pool_p_list),
            stack(c_s_list), stack(kr_s_list), stack(pool_s_list))
```

```python
import functools
import math

import jax
import jax.numpy as jnp
from jax import lax
from jax.experimental import pallas as pl
from jax.experimental.pallas import tpu as pltpu

F32 = jnp.float32
BF16 = jnp.bfloat16

LANES = 128
SUBLANES = 8
BF16_SUBLANES = 16

N_HEADS = 8
QK_NOPE = 64
QK_ROPE = 32
QK_HEAD = QK_NOPE + QK_ROPE
V_HEAD = 64
Q_LORA = 256
KV_LORA = 128
QK_CAT = KV_LORA + QK_ROPE
Q_ROWS = KV_LORA + 2 * QK_ROPE
VH_ROWS = V_HEAD + BF16_SUBLANES
CHUNK = 64
CHUNK_SHIFT = CHUNK.bit_length() - 1
ROPE_THETA = 10000.0
ATTN_SCALE = QK_HEAD ** -0.5
POOL_WIDTH = 512
POOL_GROUPS = 4
POOL_GROUP_DIM = POOL_WIDTH // POOL_GROUPS
POOL_WINDOWS = (2, 4, 8, 16)
POOL_STATE = max(POOL_WINDOWS) - 1
POOL_HALO = POOL_STATE + 1
RMS_EPS = 1e-6
LN_EPS = 1e-5
NEG_INF = -1e30
L_MIN = 2.0 ** -60
Q_SCALE = ATTN_SCALE * math.log2(math.e)

VMEM_LIMIT_BYTES = 56 * 1024 * 1024

TOKEN_TILE = 512
MIXER_TILE = 1024
X_SLOTS = 3
KV_TILE = 512
Q_TILE = 512
KV_TILES_PER_TRIP = 4
FF_CHUNKS = (768, 768, 768, 512)


def _const_spec(shape):
    nd = len(shape)
    return pl.BlockSpec(shape, lambda *_: (0,) * nd, pipeline_mode=pl.Buffered(1))


def _params(*sem):
    return pltpu.CompilerParams(dimension_semantics=sem, vmem_limit_bytes=VMEM_LIMIT_BYTES)


def _fold_q_kernel(wuk_ref, wuqn_ref, wuqr_ref, out_ref):
    half = QK_ROPE // 2
    for h in range(N_HEADS):
        lat = jnp.dot(wuk_ref[h], wuqn_ref[h], preferred_element_type=F32,
                      precision=lax.Precision.HIGHEST)
        rope = wuqr_ref[h]
        out_ref[h, 0:KV_LORA, :] = lat.astype(BF16)
        out_ref[h, KV_LORA:KV_LORA + QK_ROPE, :] = rope.astype(BF16)
        out_ref[h, KV_LORA + QK_ROPE:KV_LORA + QK_ROPE + half, :] = (-rope[half:]).astype(BF16)
        out_ref[h, KV_LORA + QK_ROPE + half:Q_ROWS, :] = rope[:half].astype(BF16)


def _fold_q(w_uq, w_uk):
    wq = w_uq.reshape(Q_LORA, N_HEADS, QK_HEAD)
    wuqn_t = jnp.transpose(wq[:, :, :QK_NOPE], (1, 2, 0))
    wuqr_t = jnp.transpose(wq[:, :, QK_NOPE:], (1, 2, 0))
    wuk_t = jnp.transpose(w_uk, (1, 0, 2))
    out = pl.pallas_call(
        _fold_q_kernel,
        out_shape=jax.ShapeDtypeStruct((N_HEADS, Q_ROWS, Q_LORA), BF16),
        name="fold_q",
    )(wuk_t, wuqn_t, wuqr_t)
    return out.reshape(N_HEADS * Q_ROWS, Q_LORA)


def _mixer_in_kernel(x_ref, w1_ref, wq_ref, wuv_ref, qg_ref, kvg_ref, cosr_ref, sinr_ref, cost_ref, sint_ref,
                     qT_ref, kcat_ref, vT_ref, ckv_ref, kr_ref, u_ref, qn_ref, ksq_ref, x_buf, x_sem,
                     *, stream_len, n_row_tiles, n_steps):
    tm = x_buf.shape[1]
    step = pl.program_id(0) * n_row_tiles + pl.program_id(1)

    def x_copy(s):
        src = x_ref.at[s // n_row_tiles, pl.ds(pl.multiple_of((s % n_row_tiles) * tm, tm), tm), :]
        return pltpu.make_async_copy(src, x_buf.at[s % X_SLOTS], x_sem.at[s % X_SLOTS])

    @pl.when(step == 0)
    def _():
        for s in range(min(X_SLOTS - 1, n_steps)):
            x_copy(s).start()

    @pl.when(step + (X_SLOTS - 1) < n_steps)
    def _():
        x_copy(step + (X_SLOTS - 1)).start()

    x_copy(step).wait()
    xb = x_buf[step % X_SLOTS].astype(BF16)
    z = jnp.dot(xb, w1_ref[...], preferred_element_type=F32)
    o1 = Q_LORA
    o2 = o1 + KV_LORA
    o3 = o2 + POOL_WIDTH
    o4 = o3 + QK_ROPE
    zq, zkv, u = z[:, :o1], z[:, o1:o2], z[:, o2:o3]
    zkr, zkr_rot = z[:, o3:o4], z[:, o4:o4 + QK_ROPE]
    u_ref[0] = u

    c = zkv * lax.rsqrt(jnp.mean(zkv * zkv, axis=-1, keepdims=True) + RMS_EPS) * kvg_ref[...]
    ckv_ref[0] = c
    kr = zkr * cosr_ref[...] + zkr_rot * sinr_ref[...]
    kr_ref[0] = kr
    cb = c.astype(BF16)
    krb = kr.astype(BF16)
    kcat_ref[0, :, 0:KV_LORA] = cb
    kcat_ref[0, :, KV_LORA:QK_CAT] = krb
    cf = cb.astype(F32)
    krf = krb.astype(F32)
    k_sq = jnp.sum(cf * cf, axis=-1, keepdims=True) + jnp.sum(krf * krf, axis=-1, keepdims=True)
    ksq_ref[0, 0] = jnp.broadcast_to(jnp.max(k_sq, axis=0, keepdims=True), ksq_ref.shape[2:])
    v_t = jnp.dot(wuv_ref[...], c.T.astype(BF16), preferred_element_type=F32)
    kv_tile = vT_ref.shape[-1]
    ones_row = lax.broadcasted_iota(jnp.int32, (VH_ROWS - V_HEAD, kv_tile), 0) == 0
    ones_rows = jnp.where(ones_row, 1.0, 0.0).astype(BF16)
    for t in range(c.shape[0] // kv_tile):
        for h in range(N_HEADS):
            vT_ref[0, t, h, 0:V_HEAD, :] = v_t[h * V_HEAD:(h + 1) * V_HEAD,
                                               t * kv_tile:(t + 1) * kv_tile].astype(BF16)
            vT_ref[0, t, h, V_HEAD:VH_ROWS, :] = ones_rows

    zqn = zq * lax.rsqrt(jnp.mean(zq * zq, axis=-1, keepdims=True) + RMS_EPS) * (qg_ref[...] * Q_SCALE)
    zqn_t = zqn.T.astype(BF16)
    q_all = jnp.dot(wq_ref[...], zqn_t, preferred_element_type=F32)
    cos_t = cost_ref[...]
    sin_t = sint_ref[...]
    for h in range(N_HEADS):
        b = h * Q_ROWS
        lat = q_all[b:b + KV_LORA].astype(BF16)
        rope = (q_all[b + KV_LORA:b + KV_LORA + QK_ROPE] * cos_t
                + q_all[b + KV_LORA + QK_ROPE:b + Q_ROWS] * sin_t).astype(BF16)
        if stream_len is None:
            qT_ref[0, h, 0:KV_LORA, :] = lat
            qT_ref[0, h, KV_LORA:QK_CAT, :] = rope
        else:
            for s in range(lat.shape[1] // stream_len):
                src = slice(s * stream_len, (s + 1) * stream_len)
                dst = slice(h * stream_len, (h + 1) * stream_len)
                qT_ref[s, 0:KV_LORA, dst] = lat[:, src]
                qT_ref[s, KV_LORA:QK_CAT, dst] = rope[:, src]
        latf = lat.astype(F32)
        ropef = rope.astype(F32)
        q_sq = jnp.sum(latf * latf, axis=0, keepdims=True) + jnp.sum(ropef * ropef, axis=0, keepdims=True)
        qn_ref[0, h:h + 1, :] = jnp.sqrt(q_sq)


def _mixer_in(x, pos, w1, wq, wuv, qg, kvg, stream_len=None):
    bsz, t, d = x.shape
    tm = MIXER_TILE
    nt = t // tm
    kv_per_tile = tm // KV_TILE
    if stream_len is None:
        q_spec = pl.BlockSpec((1, N_HEADS, QK_CAT, tm), lambda b, i: (b, 0, 0, i))
        q_shape = jax.ShapeDtypeStruct((bsz, N_HEADS, QK_CAT, t), BF16)
    else:
        per_tile = tm // stream_len
        q_spec = pl.BlockSpec((per_tile, QK_CAT, N_HEADS * stream_len), lambda b, i: (b * nt + i, 0, 0))
        q_shape = jax.ShapeDtypeStruct((bsz * t // stream_len, QK_CAT, N_HEADS * stream_len), BF16)
    half = QK_ROPE // 2
    inv = 1.0 / (ROPE_THETA ** (jnp.arange(half, dtype=F32) / half))
    ang = pos.astype(F32)[:, None] * inv[None, :]
    cos_r = jnp.tile(jnp.cos(ang), (1, 2))
    sin_r = jnp.tile(jnp.sin(ang), (1, 2))
    cos_t, sin_t = cos_r.T, sin_r.T
    n1 = w1.shape[1]
    outs = pl.pallas_call(
        functools.partial(_mixer_in_kernel, stream_len=stream_len, n_row_tiles=nt, n_steps=bsz * nt),
        grid=(bsz, nt),
        in_specs=[
            pl.BlockSpec(memory_space=pl.ANY),
            _const_spec((d, n1)),
            _const_spec((N_HEADS * Q_ROWS, Q_LORA)),
            _const_spec((N_HEADS * V_HEAD, KV_LORA)),
            _const_spec((1, Q_LORA)),
            _const_spec((1, KV_LORA)),
            pl.BlockSpec((tm, QK_ROPE), lambda b, i: (i, 0)),
            pl.BlockSpec((tm, QK_ROPE), lambda b, i: (i, 0)),
            pl.BlockSpec((QK_ROPE, tm), lambda b, i: (0, i)),
            pl.BlockSpec((QK_ROPE, tm), lambda b, i: (0, i)),
        ],
        out_specs=[
            q_spec,
            pl.BlockSpec((1, tm, QK_CAT), lambda b, i: (b, i, 0)),
            pl.BlockSpec((1, kv_per_tile, N_HEADS, VH_ROWS, KV_TILE), lambda b, i: (b, i, 0, 0, 0)),
            pl.BlockSpec((1, tm, KV_LORA), lambda b, i: (b, i, 0)),
            pl.BlockSpec((1, tm, QK_ROPE), lambda b, i: (b, i, 0)),
            pl.BlockSpec((1, tm, POOL_WIDTH), lambda b, i: (b, i, 0)),
            pl.BlockSpec((1, N_HEADS, tm), lambda b, i: (b, 0, i)),
            pl.BlockSpec((1, 1, SUBLANES, LANES), lambda b, i: (b, i, 0, 0)),
        ],
        out_shape=[
            q_shape,
            jax.ShapeDtypeStruct((bsz, t, QK_CAT), BF16),
            jax.ShapeDtypeStruct((bsz, t // KV_TILE, N_HEADS, VH_ROWS, KV_TILE), BF16),
            jax.ShapeDtypeStruct((bsz, t, KV_LORA), F32),
            jax.ShapeDtypeStruct((bsz, t, QK_ROPE), F32),
            jax.ShapeDtypeStruct((bsz, t, POOL_WIDTH), F32),
            jax.ShapeDtypeStruct((bsz, N_HEADS, t), F32),
            jax.ShapeDtypeStruct((bsz, nt, SUBLANES, LANES), F32),
        ],
        scratch_shapes=[
            pltpu.VMEM((X_SLOTS, tm, d), F32),
            pltpu.SemaphoreType.DMA((X_SLOTS,)),
        ],
        compiler_params=_params("arbitrary", "arbitrary"),
        name="mixer_in",
    )(x, w1, wq, wuv, qg, kvg, cos_r, sin_r, cos_t, sin_t)
    return outs


def _attention_kernel(kmax_ref, qT_ref, qn_ref, kcat_ref, vT_ref, oT_ref, m_ref, alpha_ref, acc_ref,
                      s_ref, mt_ref, p_ref, *, tq, tk, q_offset, n_keys, n_tiles):
    den = V_HEAD

    def values(j, h):
        return vT_ref[0, j, h]

    i = pl.program_id(1)
    q0 = q_offset + i * tq
    first_end = (q0 // CHUNK + 1) * CHUNK
    last_end = ((q0 + tq - 1) // CHUNK + 1) * CHUNK
    n_all = (jnp.minimum(last_end, n_keys) + tk - 1) // tk
    n_full = jnp.minimum(first_end // tk, n_all)
    last = N_HEADS - 1

    def scores(j, h):
        k = kcat_ref[0, pl.ds(pl.multiple_of(j * tk, tk), tk), :]
        return jnp.dot(k, qT_ref[0, h], preferred_element_type=F32)

    def visible(j):
        kchunk = lax.shift_right_arithmetic(j * tk + lax.broadcasted_iota(jnp.int32, (tk, 1), 0), CHUNK_SHIFT)
        qchunk = lax.shift_right_arithmetic(q0 + lax.broadcasted_iota(jnp.int32, (1, tq), 1), CHUNK_SHIFT)
        return kchunk <= qchunk

    def run_tiles(first_stage, tile_body, final_stage, tiles_per_trip=1):
        def loop(lo, hi, masked, per_trip=1):
            def body(t, carry):
                for r in range(per_trip):
                    tile_body(per_trip * t + r, masked)
                return carry

            lax.fori_loop(lo, hi, body, 0)

        first_stage(0)
        n_grouped = (n_full // tiles_per_trip) * tiles_per_trip
        loop(0, n_full // tiles_per_trip, False, per_trip=tiles_per_trip)
        loop(n_grouped, n_full, False)
        loop(n_full, n_all, True)
        if final_stage is not None:
            final_stage(n_all - 1)

    def stage_p(j, h, slot, masked):
        x = scores(j, h) - qn_ref[0, h:h + 1, :] * kmax_ref[pl.program_id(0)]
        if masked:
            x = jnp.where(visible(j), x, NEG_INF)
        p_ref[slot] = jnp.exp2(x).astype(BF16)

    def stage_v(j, h, slot):
        acc_ref[h] += jnp.dot(values(j, h), p_ref[slot], preferred_element_type=F32)

    def bounded_tile(j, masked):
        j_next = jnp.minimum(j + 1, n_tiles - 1)
        for h in range(N_HEADS):
            if h < last:
                stage_p(j, h + 1, (h + 1) % 2, masked)
            else:
                stage_p(j_next, 0, 0, True)
            stage_v(j, h, h % 2)

    def bounded_path():
        acc_ref[...] = jnp.zeros(acc_ref.shape, F32)
        run_tiles(lambda j: stage_p(j, 0, 0, True), bounded_tile, None,
                  tiles_per_trip=KV_TILES_PER_TRIP)

    def stage_a(j, h, slot):
        s = scores(j, h)
        s_ref[slot] = s
        mt_ref[slot] = jnp.max(s, axis=0, keepdims=True)

    def stage_b(j, h, slot, masked):
        s = s_ref[slot]
        mt = mt_ref[slot]
        if masked:
            s = jnp.where(visible(j), s, NEG_INF)
            mt = jnp.max(s, axis=0, keepdims=True)
        m_old = m_ref[h]
        m_new = jnp.maximum(m_old, mt)
        m_ref[h] = m_new
        alpha_ref[h] = jnp.exp2(m_old - m_new)
        p_ref[slot] = jnp.exp2(s - m_new).astype(BF16)

    def stage_c(j, h, slot):
        pv = jnp.dot(values(j, h), p_ref[slot], preferred_element_type=F32)
        acc_ref[h] = alpha_ref[h] * acc_ref[h] + pv

    def exact_tile(j, masked):
        j_next = jnp.minimum(j + 1, n_tiles - 1)
        j_prev = jnp.maximum(j - 1, 0)
        for h in range(N_HEADS):
            if h < last:
                stage_a(j, h + 1, (h + 1) % 2)
            else:
                stage_a(j_next, 0, 0)
            stage_b(j, h, h % 2, masked)
            if h > 0:
                stage_c(j, h - 1, (h - 1) % 2)
            else:
                stage_c(j_prev, last, last % 2)

    def exact_path():
        m_ref[...] = jnp.full(m_ref.shape, NEG_INF, F32)
        acc_ref[...] = jnp.zeros(acc_ref.shape, F32)
        p_ref[last % 2] = jnp.zeros((tk, tq), BF16)
        alpha_ref[last] = jnp.ones((1, tq), F32)
        run_tiles(lambda j: stage_a(j, 0, 0), exact_tile,
                  lambda j: stage_c(j, last, last % 2))

    bounded_path()
    denom = jnp.concatenate([acc_ref[h, den:den + 1, :] for h in range(N_HEADS)], axis=0)
    pl.when(jnp.logical_not(jnp.min(denom) >= L_MIN))(exact_path)

    for h in range(N_HEADS):
        o_t = acc_ref[h, 0:den, :] / acc_ref[h, den:den + 1, :]
        oT_ref[0, h * V_HEAD:(h + 1) * V_HEAD, :] = o_t.astype(BF16)


def _attention(qT, qn, kmax, kcat, vT, *, tq, tk, q_offset, n_keys):
    bsz, _, _, t_q = qT.shape
    t_k = kcat.shape[1]
    n_tiles = t_k // tk
    kern = functools.partial(_attention_kernel, tq=tq, tk=tk, q_offset=q_offset, n_keys=n_keys,
                             n_tiles=n_tiles)
    resident = pl.Buffered(1)
    return pl.pallas_call(
        kern,
        grid=(bsz, t_q // tq),
        in_specs=[
            pl.BlockSpec(memory_space=pltpu.SMEM),
            pl.BlockSpec((1, N_HEADS, QK_CAT, tq), lambda b, i: (b, 0, 0, i)),
            pl.BlockSpec((1, N_HEADS, tq), lambda b, i: (b, 0, i)),
            pl.BlockSpec((1, t_k, QK_CAT), lambda b, i: (b, 0, 0), pipeline_mode=resident),
            pl.BlockSpec((1, n_tiles, N_HEADS, VH_ROWS, tk), lambda b, i: (b, 0, 0, 0, 0),
                         pipeline_mode=resident),
        ],
        out_specs=pl.BlockSpec((1, N_HEADS * V_HEAD, tq), lambda b, i: (b, 0, i)),
        out_shape=jax.ShapeDtypeStruct((bsz, N_HEADS * V_HEAD, t_q), BF16),
        scratch_shapes=[
            pltpu.VMEM((N_HEADS, 1, tq), F32),
            pltpu.VMEM((N_HEADS, 1, tq), F32),
            pltpu.VMEM((N_HEADS, VH_ROWS, tq), F32),
            pltpu.VMEM((2, tk, tq), F32),
            pltpu.VMEM((2, 1, tq), F32),
            pltpu.VMEM((2, tk, tq), BF16),
        ],
        compiler_params=_params("parallel", "arbitrary"),
        name="attention",
    )(kmax, qT, qn, kcat, vT)


def _sample_attention_kernel(qT_ref, cc_ref, ckr_ref, cn_ref, krn_ref, wuvT_ref, oT_ref, *, n_s, past):
    lanes = N_HEADS * n_s
    q_t = qT_ref[0]
    cc = cc_ref[0].astype(BF16)
    cn = cn_ref[0].astype(BF16)
    k_cached = jnp.concatenate([cc, ckr_ref[0].astype(BF16)], axis=1)
    k_new = jnp.concatenate([cn, krn_ref[0].astype(BF16)], axis=1)
    s_c = jnp.dot(k_cached, q_t, preferred_element_type=F32)
    s_n = jnp.dot(k_new, q_t, preferred_element_type=F32)

    q_idx = lax.rem(lax.broadcasted_iota(jnp.int32, (1, lanes), 1), n_s)
    qchunk = lax.shift_right_arithmetic(past + q_idx, CHUNK_SHIFT)
    kchunk_c = lax.shift_right_arithmetic(lax.broadcasted_iota(jnp.int32, (past, 1), 0), CHUNK_SHIFT)
    kchunk_n = lax.shift_right_arithmetic(past + lax.broadcasted_iota(jnp.int32, (n_s, 1), 0), CHUNK_SHIFT)
    s_c = jnp.where(kchunk_c <= qchunk, s_c, NEG_INF)
    s_n = jnp.where(kchunk_n <= qchunk, s_n, NEG_INF)

    m = jnp.maximum(jnp.max(s_c, axis=0, keepdims=True), jnp.max(s_n, axis=0, keepdims=True))
    p_c = jnp.exp2(s_c - m)
    p_n = jnp.exp2(s_n - m)
    denom = jnp.sum(p_c, axis=0, keepdims=True) + jnp.sum(p_n, axis=0, keepdims=True)
    contract_rows = (((0,), (0,)), ((), ()))
    att_t = (lax.dot_general(cc, p_c.astype(BF16), contract_rows, preferred_element_type=F32)
             + lax.dot_general(cn, p_n.astype(BF16), contract_rows, preferred_element_type=F32)) / denom
    att_t = att_t.astype(BF16)
    for h in range(N_HEADS):
        o_t = jnp.dot(wuvT_ref[h], att_t[:, h * n_s:(h + 1) * n_s], preferred_element_type=F32)
        oT_ref[0, h * V_HEAD:(h + 1) * V_HEAD, :] = o_t.astype(BF16)


def _sample_attention(qT, cache_c, cache_kr, c_new, kr_new, wuvT):
    bsz, past, _ = cache_c.shape
    n_s = c_new.shape[1]
    kern = functools.partial(_sample_attention_kernel, n_s=n_s, past=past)
    return pl.pallas_call(
        kern,
        grid=(bsz,),
        in_specs=[
            pl.BlockSpec((1, QK_CAT, N_HEADS * n_s), lambda b: (b, 0, 0)),
            pl.BlockSpec((1, past, KV_LORA), lambda b: (b, 0, 0)),
            pl.BlockSpec((1, past, QK_ROPE), lambda b: (b, 0, 0)),
            pl.BlockSpec((1, n_s, KV_LORA), lambda b: (b, 0, 0)),
            pl.BlockSpec((1, n_s, QK_ROPE), lambda b: (b, 0, 0)),
            _const_spec((N_HEADS, V_HEAD, KV_LORA)),
        ],
        out_specs=pl.BlockSpec((1, N_HEADS * V_HEAD, n_s), lambda b: (b, 0, 0)),
        out_shape=jax.ShapeDtypeStruct((bsz, N_HEADS * V_HEAD, n_s), BF16),
        compiler_params=_params("parallel"),
        name="sample_attention",
    )(qT, cache_c, cache_kr, c_new, kr_new, wuvT)


def _layer_norm(r, g, b):
    mu = jnp.mean(r, axis=-1, keepdims=True)
    cen = r - mu
    var = jnp.mean(cen * cen, axis=-1, keepdims=True)
    return cen * lax.rsqrt(var + LN_EPS) * g + b


def _finish_kernel(x_ref, oT_ref, u_ref, pre_ref, wg_ref, bg_ref, wup_ref, wpool_ref, pscale_ref, wo_ref,
                g1_ref, b1_ref, wgate_ref, wupf_ref, wdown_ref, g2_ref, b2_ref, y_ref, ext_ref, d_ref,
                *, seg_len, n_seg, pos0_fn, alpha):
    d_model = x_ref.shape[-1]
    x = x_ref[...]
    xb = x.astype(BF16)
    zg = jnp.dot(xb, wg_ref[...], preferred_element_type=F32) + bg_ref[...]
    gates = 1.0 / (1.0 + jnp.exp(-zg))

    pos0 = pos0_fn(pl.program_id(0))
    row = lax.broadcasted_iota(jnp.int32, (seg_len, 1), 0)
    for s in range(n_seg):
        ext_ref[0:POOL_HALO, :] = pre_ref[s]
        ext_ref[POOL_HALO:POOL_HALO + seg_len, :] = u_ref[s * seg_len:(s + 1) * seg_len, :]
        for g, w in enumerate(POOL_WINDOWS):
            lanes = slice(g * POOL_GROUP_DIM, (g + 1) * POOL_GROUP_DIM)
            cur = ext_ref[POOL_HALO:POOL_HALO + seg_len, lanes]
            acc = cur
            for k in range(1, w):
                acc = acc + ext_ref[POOL_HALO - k:POOL_HALO - k + seg_len, lanes]
            cnt = jnp.minimum(pos0 + row + 1, w).astype(F32)
            d_ref[s * seg_len:(s + 1) * seg_len, lanes] = (acc * (1.0 / cnt) - cur).astype(BF16)

    pool_out = d_model // POOL_GROUPS
    parts = []
    for g in range(POOL_GROUPS):
        dg = d_ref[:, g * POOL_GROUP_DIM:(g + 1) * POOL_GROUP_DIM]
        parts.append(jnp.dot(dg, wpool_ref[g], preferred_element_type=F32))
    p_branch = jnp.concatenate(parts, axis=-1) * pscale_ref[...]

    wup = wup_ref[...]
    a_branch = jnp.concatenate(
        [lax.dot_general(oT_ref[s], wup, (((0,), (0,)), ((), ())), preferred_element_type=F32)
         for s in range(n_seg)], axis=0)
    m = gates[:, :d_model] * a_branch + gates[:, d_model:] * p_branch
    r = alpha * x + jnp.dot(m.astype(BF16), wo_ref[...], preferred_element_type=F32)
    h = _layer_norm(r, g1_ref[...], b1_ref[...])

    hb = h.astype(BF16)
    f = None
    lo = 0
    for width in FF_CHUNKS:
        gate = jnp.dot(hb, wgate_ref[:, lo:lo + width], preferred_element_type=F32)
        up = jnp.dot(hb, wupf_ref[:, lo:lo + width], preferred_element_type=F32)
        act = (gate / (1.0 + jnp.exp(-gate)) * up).astype(BF16)
        part = jnp.dot(act, wdown_ref[lo:lo + width, :], preferred_element_type=F32)
        f = part if f is None else f + part
        lo += width
    y_ref[...] = _layer_norm(alpha * h + f, g2_ref[...], b2_ref[...])


def _finish(x2, oT, u2, prefix, wg, bg, wup, wpool, pscale, wo, g1, b1, wgate, wupf, wdown, g2, b2,
         *, seg_len, pos0_fn, alpha):
    n, d = x2.shape
    tm = TOKEN_TILE
    n_seg = tm // seg_len
    _, hv, t_o = oT.shape
    assert sum(FF_CHUNKS) == wgate.shape[1]
    if n_seg == 1:
        tiles_per_row = t_o // tm
        o_spec = pl.BlockSpec((1, hv, tm), lambda i: (i // tiles_per_row, 0, i % tiles_per_row))
    else:
        assert t_o == seg_len
        o_spec = pl.BlockSpec((n_seg, hv, seg_len), lambda i: (i, 0, 0))
    kern = functools.partial(_finish_kernel, seg_len=seg_len, n_seg=n_seg, pos0_fn=pos0_fn, alpha=alpha)
    return pl.pallas_call(
        kern,
        grid=(n // tm,),
        in_specs=[
            pl.BlockSpec((tm, d), lambda i: (i, 0)),
            o_spec,
            pl.BlockSpec((tm, POOL_WIDTH), lambda i: (i, 0)),
            pl.BlockSpec((n_seg, POOL_HALO, POOL_WIDTH), lambda i: (i, 0, 0)),
            _const_spec(wg.shape), _const_spec(bg.shape), _const_spec(wup.shape),
            _const_spec(wpool.shape), _const_spec(pscale.shape), _const_spec(wo.shape),
            _const_spec(g1.shape), _const_spec(b1.shape),
            _const_spec(wgate.shape), _const_spec(wupf.shape), _const_spec(wdown.shape),
            _const_spec(g2.shape), _const_spec(b2.shape),
        ],
        out_specs=pl.BlockSpec((tm, d), lambda i: (i, 0)),
        out_shape=jax.ShapeDtypeStruct((n, d), F32),
        scratch_shapes=[
            pltpu.VMEM((POOL_HALO + seg_len, POOL_WIDTH), F32),
            pltpu.VMEM((tm, POOL_WIDTH), BF16),
        ],
        compiler_params=_params("parallel"),
        name="finish",
    )(x2, oT, u2, prefix, wg, bg, wup, wpool, pscale, wo, g1, b1, wgate, wupf, wdown, g2, b2)


def _prompt_pos0(tiles_per_seq, i):
    return (i % tiles_per_seq) * TOKEN_TILE


def _sample_pos0(past, i):
    return past


def kernel(x_prompt, x_sample, cache_kv_latent, cache_k_rope, state_pool, w_in, b_gate, q_norm_g, w_uq,
           kv_norm_g, w_uk, w_uv, w_attn_up, w_pool, pool_scale, w_o, ln1_g, ln1_b, w_gate, w_up, w_down,
           ln2_g, ln2_b):
    depth = w_in.shape[0]
    bp, n_p, d = x_prompt.shape
    bs, n_s, _ = x_sample.shape
    past = cache_kv_latent.shape[2]
    alpha = (2 * depth) ** 0.25
    half = QK_ROPE // 2
    o1 = Q_LORA
    o2 = o1 + KV_LORA
    o3 = o2 + QK_ROPE
    o4 = o3 + POOL_WIDTH

    pos_p = jnp.arange(n_p, dtype=jnp.int32)
    pos_s = past + (jnp.arange(bs * n_s, dtype=jnp.int32) % n_s)

    h_p, h_s = x_prompt, x_sample
    c_p_list, kr_p_list, pool_p_list = [], [], []
    c_s_list, kr_s_list, pool_s_list = [], [], []
    for l in range(depth):
        wl = w_in[l]
        w_kr = wl[:, o2:o3]
        w_kr_rot = jnp.concatenate([-w_kr[:, half:], w_kr[:, :half]], axis=1)
        w1 = jnp.concatenate([wl[:, :o2], wl[:, o3:o4], w_kr, w_kr_rot], axis=1).astype(BF16)
        wg = wl[:, o4:].astype(BF16)
        bg = b_gate[l][None, :]
        wq = _fold_q(w_uq[l], w_uk[l])
        qg = q_norm_g[l][None, :]
        kvg = kv_norm_g[l][None, :]
        wuvT = jnp.transpose(w_uv[l], (1, 2, 0)).astype(BF16)
        wuv = wuvT.reshape(N_HEADS * V_HEAD, KV_LORA)
        wup = w_attn_up[l].astype(BF16)
        wpool = w_pool[l].astype(BF16)
        pscale = pool_scale[l][None, :]
        wo = w_o[l].astype(BF16)
        g1, b1 = ln1_g[l][None, :], ln1_b[l][None, :]
        g2, b2 = ln2_g[l][None, :], ln2_b[l][None, :]
        wgate, wupf, wdown = w_gate[l].astype(BF16), w_up[l].astype(BF16), w_down[l].astype(BF16)

        qT, kcat, vT, c, kr, u, qn, ksq = _mixer_in(h_p, pos_p, w1, wq, wuv, qg, kvg)
        kmax = jnp.sqrt(jnp.max(ksq[:, :, 0, 0], axis=1))
        oT = _attention(qT, qn, kmax, kcat, vT, tq=Q_TILE, tk=KV_TILE, q_offset=0, n_keys=n_p)
        tiles_per_seq = n_p // TOKEN_TILE
        tails = u.reshape(bp, tiles_per_seq, TOKEN_TILE, POOL_WIDTH)[:, :, TOKEN_TILE - POOL_HALO:, :]
        prefix = jnp.concatenate([jnp.zeros_like(tails[:, :1]), tails[:, :-1]], axis=1)
        prefix = prefix.reshape(bp * tiles_per_seq, POOL_HALO, POOL_WIDTH)
        y = _finish(h_p.reshape(bp * n_p, d), oT, u.reshape(bp * n_p, POOL_WIDTH), prefix,
                 wg, bg, wup, wpool, pscale, wo, g1, b1, wgate, wupf, wdown, g2, b2, seg_len=TOKEN_TILE,
                 pos0_fn=functools.partial(_prompt_pos0, tiles_per_seq), alpha=alpha)
        c_p_list.append(c)
        kr_p_list.append(kr)
        pool_p_list.append(u[:, n_p - POOL_STATE:])
        h_p = y.reshape(bp, n_p, d)

        xs = h_s.reshape(1, bs * n_s, d)
        qT, _, _, c, kr, u, _, _ = _mixer_in(xs, pos_s, w1, wq, wuv, qg, kvg, stream_len=n_s)
        c = c.reshape(bs, n_s, KV_LORA)
        kr = kr.reshape(bs, n_s, QK_ROPE)
        u = u.reshape(bs, n_s, POOL_WIDTH)
        oT = _sample_attention(qT, cache_kv_latent[l], cache_k_rope[l], c, kr, wuvT)
        prefix = jnp.pad(state_pool[l], ((0, 0), (POOL_HALO - POOL_STATE, 0), (0, 0)))
        y = _finish(h_s.reshape(bs * n_s, d), oT, u.reshape(bs * n_s, POOL_WIDTH), prefix,
                 wg, bg, wup, wpool, pscale, wo, g1, b1, wgate, wupf, wdown, g2, b2, seg_len=n_s,
                 pos0_fn=functools.partial(_sample_pos0, past), alpha=alpha)
        u_all = jnp.concatenate([state_pool[l], u], axis=1)
        c_s_list.append(c)
        kr_s_list.append(kr)
        pool_s_list.append(u_all[:, u_all.shape[1] - POOL_STATE:])
        h_s = y.reshape(bs, n_s, d)

    def stack(parts):
        return parts[0][None] if len(parts) == 1 else jnp.stack(parts)

    return (h_p, h_s, stack(c_p_list), stack(kr_p_list), stack(pool_p_list),
            stack(c_s_list), stack(kr_s_list), stack(pool_s_list))
```
